```python
import jax, jax.numpy as jnp
from jax import lax
import numpy as np

D_MODEL = 1024
BATCH = 2
SEQ = 8192
DEPTH = 2
DEC_BATCH = 32
DEC_SEQ = 4
PAST_LEN = 8192
PAGE_SIZE = 128

N_MIXERS = 2
N_HEADS = 8
HEAD_DIM = 128
WIDTH = N_HEADS * HEAD_DIM
ROT_DIM = HEAD_DIM // 4
ROPE_THETA = 500000.0
EPS = 1e-6
Q_BLOCK = 64
ATTN_SCALE = HEAD_DIM ** -0.5
A_KV_HEADS = 4
GA = N_HEADS // A_KV_HEADS
IDX_HEADS = 8
IDX_DIM = 64
IDX_ROT = IDX_DIM // 4
TOPK_MAX = 256
A_IN = WIDTH + 2 * A_KV_HEADS * HEAD_DIM + IDX_HEADS * IDX_DIM + IDX_DIM + IDX_HEADS + WIDTH
B_KV_HEADS = 2
GB = N_HEADS // B_KV_HEADS
CMP_BLOCK = 32
CMP_STRIDE = 16
SLC_BLOCK = 64
SLC_TOPN = 16
WINDOW = 512
B_KVW = 2 * B_KV_HEADS * HEAD_DIM
B_IN = WIDTH + 3 * B_KVW + 3 * N_HEADS + WIDTH

kernel_name = 'hybrid_dsa_nsa_decode_step'


def rmsnorm(x, g):
    xf = x.astype(jnp.float32)
    y = xf * lax.rsqrt(jnp.mean(xf * xf, axis=-1, keepdims=True) + EPS)
    return (y * g.astype(jnp.float32)).astype(x.dtype)


def rope(x, pos, rot_dim):
    half = rot_dim // 2
    freq = ROPE_THETA ** (-jnp.arange(half, dtype=jnp.float32) / half)
    ang = pos.astype(jnp.float32)[:, None] * freq
    ang = ang.reshape((pos.shape[0],) + (1,) * (x.ndim - 3) + (half,))
    cos, sin = jnp.cos(ang), jnp.sin(ang)
    xr = x[..., :rot_dim].astype(jnp.float32)
    x1, x2 = xr[..., :half], xr[..., half:]
    rot = jnp.concatenate([x1 * cos - x2 * sin, x2 * cos + x1 * sin], axis=-1).astype(x.dtype)
    return jnp.concatenate([rot, x[..., rot_dim:]], axis=-1)


def masked_softmax(s, mask):
    p = jax.nn.softmax(jnp.where(mask, s, -1e30), axis=-1)
    return jnp.where(mask, p, 0.0)


def split_cols(u, sizes):
    return jnp.split(u, np.cumsum(sizes)[:-1].tolist(), axis=-1)


def to_blocks(a):
    b, s = a.shape[:2]
    return a.reshape((b, s // Q_BLOCK, Q_BLOCK) + a.shape[2:]).swapaxes(0, 1)


def from_blocks(a):
    return a.swapaxes(0, 1).reshape((a.shape[1], -1) + a.shape[3:])


def out_proj(x, o, z, w_o):
    b, t = x.shape[:2]
    return x + (o.reshape(b, t, WIDTH) * jax.nn.silu(z)) @ w_o


def gather_past(pool, page_table):
    g = pool[page_table]
    return g.reshape((page_table.shape[0], -1) + pool.shape[2:])


def gather_rows(pool, page_table, new_rows, idx, head_idx=None):
    bd, n_pages = page_table.shape
    past_len = n_pages * PAGE_SIZE
    flat = pool.reshape((-1,) + pool.shape[2:])
    b = jnp.arange(bd).reshape((bd,) + (1,) * (idx.ndim - 1))
    p_idx = jnp.minimum(idx, past_len - 1)
    phys = page_table[b, p_idx // PAGE_SIZE] * PAGE_SIZE + p_idx % PAGE_SIZE
    n_idx = jnp.clip(idx - past_len, 0, new_rows.shape[1] - 1)
    if head_idx is None:
        past, new = flat[phys], new_rows[b, n_idx]
    else:
        past, new = flat[phys, head_idx], new_rows[b, n_idx, head_idx]
    is_past = (idx < past_len).reshape(idx.shape + (1,) * (past.ndim - idx.ndim))
    return jnp.where(is_past, past, new)


def proj_a(x, pos, norm_g, w_in, q_g, k_g):
    b, t, _ = x.shape
    u = rmsnorm(x, norm_g) @ w_in
    q, kv, iq, ik, iw, z = split_cols(u, [WIDTH, 2 * A_KV_HEADS * HEAD_DIM, IDX_HEADS * IDX_DIM, IDX_DIM, IDX_HEADS, WIDTH])
    q = rope(rmsnorm(q.reshape(b, t, N_HEADS, HEAD_DIM), q_g), pos, ROT_DIM)
    kv = kv.reshape(b, t, A_KV_HEADS, 2, HEAD_DIM)
    k = rope(rmsnorm(kv[..., 0, :], k_g), pos, ROT_DIM)
    kv = jnp.stack([k, kv[..., 1, :]], axis=-2)
    iq = rope(iq.reshape(b, t, IDX_HEADS, IDX_DIM), pos, IDX_ROT)
    ik = rope(ik, pos, IDX_ROT)
    iw = iw * (IDX_HEADS * IDX_DIM) ** -0.5
    return q, kv, iq, ik, iw, z


def dsa_core(q, q_idx, w_idx, k_idx_all, t_pos, n_keep, gather_kv):
    L = k_idx_all.shape[1]
    logits = jnp.einsum('bqhd,bsd->bqhs', q_idx, k_idx_all, preferred_element_type=jnp.float32)
    score = jnp.einsum('bqhs,bqh->bqs', jax.nn.relu(logits), w_idx.astype(jnp.float32))
    causal = jnp.arange(L)[None, :] <= t_pos[:, None]
    score = jnp.where(causal[None], score, -jnp.inf)
    _, sel = lax.top_k(score, n_keep)
    kv = gather_kv(sel)
    b, nq, _, _ = q.shape
    qg = q.reshape(b, nq, A_KV_HEADS, GA, HEAD_DIM)
    s = jnp.einsum('bqgrd,bqkgd->bqgrk', qg, kv[..., 0, :], preferred_element_type=jnp.float32) * ATTN_SCALE
    valid = (sel <= t_pos[None, :, None])[:, :, None, None, :]
    p = masked_softmax(s, valid)
    o = jnp.einsum('bqgrk,bqkgd->bqgrd', p.astype(q.dtype), kv[..., 1, :])
    return o.reshape(b, nq, N_HEADS, HEAD_DIM)


def dsa_prompt(x, norm_g, w_in, q_g, k_g, w_o):
    b, s, _ = x.shape
    pos = jnp.arange(s)
    q, kv, iq, ik, iw, z = proj_a(x, pos, norm_g, w_in, q_g, k_g)
    n_keep = min(TOPK_MAX, s // 4)
    bi = jnp.arange(b)[:, None, None]

    def block(args):
        qb, iqb, iwb, tb = args
        return dsa_core(qb, iqb, iwb, ik, tb, n_keep, lambda sel: kv[bi, sel])

    o = lax.map(block, (to_blocks(q), to_blocks(iq), to_blocks(iw), pos.reshape(-1, Q_BLOCK)))
    return out_proj(x, from_blocks(o), z, w_o), kv, ik


def dsa_sample(x, cache_kv, cache_idx, page_table, norm_g, w_in, q_g, k_g, w_o):
    b, t, _ = x.shape
    past_len = page_table.shape[1] * PAGE_SIZE
    pos = past_len + jnp.arange(t)
    q, kv_new, iq, ik_new, iw, z = proj_a(x, pos, norm_g, w_in, q_g, k_g)
    ik_all = jnp.concatenate([gather_past(cache_idx, page_table), ik_new], axis=1)
    n_keep = min(TOPK_MAX, (past_len + t) // 4)
    o = dsa_core(q, iq, iw, ik_all, pos, n_keep, lambda sel: gather_rows(cache_kv, page_table, kv_new, sel))
    return out_proj(x, o, z, w_o), kv_new, ik_new


def proj_b(x, pos, norm_g, w_in, q_g, k_g):
    b, t, _ = x.shape
    u = rmsnorm(x, norm_g) @ w_in
    q, kvc, kvs, kvw, gates, z = split_cols(u, [WIDTH, B_KVW, B_KVW, B_KVW, 3 * N_HEADS, WIDTH])
    q = rope(rmsnorm(q.reshape(b, t, N_HEADS, HEAD_DIM), q_g), pos, ROT_DIM)
    branches = []
    for br, kv in enumerate((kvc, kvs, kvw)):
        kv = kv.reshape(b, t, B_KV_HEADS, 2, HEAD_DIM)
        k = rope(rmsnorm(kv[..., 0, :], k_g[br]), pos, ROT_DIM)
        branches.append(jnp.stack([k, kv[..., 1, :]], axis=-2))
    gates = jax.nn.sigmoid(gates.reshape(b, t, N_HEADS, 3))
    return q, branches[0], branches[1], branches[2], gates, z


def compress(kv, w_cmp, pe_cmp):
    b, L = kv.shape[:2]
    r = CMP_BLOCK // CMP_STRIDE
    nc = -(-L // CMP_STRIDE)
    kv = jnp.pad(kv, ((0, 0), (0, nc * CMP_STRIDE - L), (0, 0), (0, 0), (0, 0)))
    ch = kv.reshape((b, nc, CMP_STRIDE) + kv.shape[2:])
    n_cmp = nc - r + 1
    out = 0.0
    for j in range(r):
        sl = slice(j * CMP_STRIDE, (j + 1) * CMP_STRIDE)
        part = jnp.einsum('bnlgcd,lcde->bngce', ch + pe_cmp[sl][:, None], w_cmp[sl])
        out = out + part[:, j:j + n_cmp]
    return out


def slc_overlap(n_cmp, n_slc):
    i = np.arange(n_cmp)[:, None]
    j = np.arange(n_slc)[None, :]
    lo = np.maximum(i * CMP_STRIDE, j * SLC_BLOCK)
    hi = np.minimum(i * CMP_STRIDE + CMP_BLOCK, (j + 1) * SLC_BLOCK)
    return jnp.asarray(np.maximum(hi - lo, 0) / CMP_STRIDE, dtype=jnp.float32)


def cmp_branch(qg, kv_cmp, t_pos):
    n = kv_cmp.shape[1]
    s = jnp.einsum('bqgrd,bngd->bqgrn', qg, kv_cmp[..., 0, :], preferred_element_type=jnp.float32) * ATTN_SCALE
    valid = (jnp.arange(n) * CMP_STRIDE + CMP_BLOCK - 1)[None, :] <= t_pos[:, None]
    p = masked_softmax(s, valid[None, :, None, None, :])
    o = jnp.einsum('bqgrn,bngd->bqgrd', p.astype(qg.dtype), kv_cmp[..., 1, :])
    return o, p.sum(axis=3)


def slc_branch(qg, p_grp, t_pos, L, gather_fn):
    n_cmp = p_grp.shape[-1]
    n_slc = -(-L // SLC_BLOCK)
    imp = jnp.einsum('bqgn,nj->bqgj', p_grp, slc_overlap(n_cmp, n_slc))
    j = jnp.arange(n_slc)[None, :]
    cur = (t_pos // SLC_BLOCK)[:, None]
    admissible = j * SLC_BLOCK <= t_pos[:, None]
    forced = (j == 0) | (j == cur) | (j == cur - 1)
    imp = jnp.where(forced[None, :, None], jnp.inf, jnp.where(admissible[None, :, None], imp, -jnp.inf))
    _, blk = lax.top_k(imp, min(SLC_TOPN, n_slc))
    pos = (blk[..., None] * SLC_BLOCK + jnp.arange(SLC_BLOCK)).reshape(blk.shape[:3] + (-1,))
    kv = gather_fn(pos)
    s = jnp.einsum('bqgrd,bqgkd->bqgrk', qg, kv[..., 0, :], preferred_element_type=jnp.float32) * ATTN_SCALE
    valid = (pos <= t_pos[None, :, None, None])[:, :, :, None, :]
    p = masked_softmax(s, valid)
    return jnp.einsum('bqgrk,bqgkd->bqgrd', p.astype(qg.dtype), kv[..., 1, :])


def win_branch(qg, kv, kpos, t_pos):
    s = jnp.einsum('bqgrd,bkgd->bqgrk', qg, kv[..., 0, :], preferred_element_type=jnp.float32) * ATTN_SCALE
    kp, tp = kpos[None, :], t_pos[:, None]
    valid = (kp <= tp) & (kp > tp - WINDOW) & (kp >= 0)
    p = masked_softmax(s, valid[None, :, None, None, :])
    return jnp.einsum('bqgrk,bkgd->bqgrd', p.astype(qg.dtype), kv[..., 1, :])


def nsa_mix(q, gates, t_pos, kv_cmp, L, gather_slc, kv_win, kpos_win):
    b, nq = q.shape[:2]
    qg = q.reshape(b, nq, B_KV_HEADS, GB, HEAD_DIM)
    o_c, p_grp = cmp_branch(qg, kv_cmp, t_pos)
    o_s = slc_branch(qg, p_grp, t_pos, L, gather_slc)
    o_w = win_branch(qg, kv_win, kpos_win, t_pos)
    g = gates.reshape(b, nq, B_KV_HEADS, GB, 3)
    o = g[..., 0:1] * o_c + g[..., 1:2] * o_s + g[..., 2:3] * o_w
    return o.reshape(b, nq, N_HEADS, HEAD_DIM)


def nsa_prompt(x, norm_g, w_in, q_g, k_g, w_cmp, pe_cmp, w_o):
    b, s, _ = x.shape
    pos = jnp.arange(s)
    q, kvc, kvs, kvw, gates, z = proj_b(x, pos, norm_g, w_in, q_g, k_g)
    kv_cmp = compress(kvc, w_cmp, pe_cmp)
    kvw_pad = jnp.pad(kvw, ((0, 0), (WINDOW, 0), (0, 0), (0, 0), (0, 0)))
    bi = jnp.arange(b)[:, None, None, None]
    hi = jnp.arange(B_KV_HEADS)[None, None, :, None]

    def block(args):
        qb, gb, tb = args
        start = tb[0]
        kw = lax.dynamic_slice_in_dim(kvw_pad, start, WINDOW + Q_BLOCK, axis=1)
        kpos = start - WINDOW + jnp.arange(WINDOW + Q_BLOCK)
        return nsa_mix(qb, gb, tb, kv_cmp, s, lambda p: kvs[bi, p, hi], kw, kpos)

    o = lax.map(block, (to_blocks(q), to_blocks(gates), pos.reshape(-1, Q_BLOCK)))
    y = out_proj(x, from_blocks(o), z, w_o)
    return y, kvc, kvs, kvw[:, s - min(WINDOW, s):]


def nsa_sample(x, cache_cmp, cache_slc, state_win, page_table, norm_g, w_in, q_g, k_g, w_cmp, pe_cmp, w_o):
    b, t, _ = x.shape
    past_len = page_table.shape[1] * PAGE_SIZE
    pos = past_len + jnp.arange(t)
    q, kvc_new, kvs_new, kvw_new, gates, z = proj_b(x, pos, norm_g, w_in, q_g, k_g)
    kvc_all = jnp.concatenate([gather_past(cache_cmp, page_table), kvc_new], axis=1)
    kv_cmp = compress(kvc_all, w_cmp, pe_cmp)
    hi = jnp.arange(B_KV_HEADS)[None, None, :, None]
    wb = state_win.shape[1]
    kw = jnp.concatenate([state_win, kvw_new], axis=1)
    kpos = past_len - wb + jnp.arange(wb + t)
    o = nsa_mix(q, gates, pos, kv_cmp, past_len + t,
                lambda p: gather_rows(cache_slc, page_table, kvs_new, p, hi), kw, kpos)
    return out_proj(x, o, z, w_o), kvc_new, kvs_new, kw[:, t:]


def setup_inputs(seed: int = 0) -> dict:
    key = jax.random.key(seed)
    ks = jax.random.split(key, 24)
    nrm = jax.random.normal
    f32 = jnp.float32
    n_pages = PAST_LEN // PAGE_SIZE
    n_used = DEC_BATCH * n_pages
    n_pool = n_used + n_used // 4
    win_buf = min(WINDOW, PAST_LEN)
    page_table = jax.random.permutation(ks[7], n_pool)[:n_used].reshape(DEC_BATCH, n_pages).astype(jnp.int32)
    return {
        'x_prompt': nrm(ks[0], (BATCH, SEQ, D_MODEL), f32),
        'x_sample': nrm(ks[1], (DEC_BATCH, DEC_SEQ, D_MODEL), f32),
        'cache_a_kv': nrm(ks[2], (n_pool, PAGE_SIZE, A_KV_HEADS, 2, HEAD_DIM), f32),
        'cache_a_idx': nrm(ks[3], (n_pool, PAGE_SIZE, IDX_DIM), f32),
        'cache_b_cmp_kv': nrm(ks[4], (n_pool, PAGE_SIZE, B_KV_HEADS, 2, HEAD_DIM), f32),
        'cache_b_slc_kv': nrm(ks[5], (n_pool, PAGE_SIZE, B_KV_HEADS, 2, HEAD_DIM), f32),
        'state_b_win_kv': nrm(ks[6], (DEC_BATCH, win_buf, B_KV_HEADS, 2, HEAD_DIM), f32),
        'page_table': page_table,
        'a_norm': 1.0 + 0.05 * nrm(ks[8], (D_MODEL,), f32),
        'a_w_in': nrm(ks[9], (D_MODEL, A_IN), f32) * D_MODEL ** -0.5,
        'a_q_norm': 1.0 + 0.05 * nrm(ks[10], (HEAD_DIM,), f32),
        'a_k_norm': 1.0 + 0.05 * nrm(ks[11], (HEAD_DIM,), f32),
        'a_w_o': nrm(ks[12], (WIDTH, D_MODEL), f32) * WIDTH ** -0.5,
        'b_norm': 1.0 + 0.05 * nrm(ks[13], (D_MODEL,), f32),
        'b_w_in': nrm(ks[14], (D_MODEL, B_IN), f32) * D_MODEL ** -0.5,
        'b_q_norm': 1.0 + 0.05 * nrm(ks[15], (HEAD_DIM,), f32),
        'b_k_norm': 1.0 + 0.05 * nrm(ks[16], (3, HEAD_DIM), f32),
        'b_cmp_w': nrm(ks[17], (CMP_BLOCK, 2, HEAD_DIM, HEAD_DIM), f32) * (CMP_BLOCK * HEAD_DIM) ** -0.5,
        'b_cmp_pe': 0.5 * nrm(ks[18], (CMP_BLOCK, 2, HEAD_DIM), f32),
        'b_w_o': nrm(ks[19], (WIDTH, D_MODEL), f32) * WIDTH ** -0.5,
    }


def reference(x_prompt, x_sample, cache_a_kv, cache_a_idx, cache_b_cmp_kv, cache_b_slc_kv, state_b_win_kv,
              page_table, a_norm, a_w_in, a_q_norm, a_k_norm, a_w_o, b_norm, b_w_in, b_q_norm, b_k_norm,
              b_cmp_w, b_cmp_pe, b_w_o):
    yp, ys = x_prompt, x_sample
    for i in range(DEPTH):
        if i % N_MIXERS == 0:
            yp, a_kv_p, a_idx_p = dsa_prompt(yp, a_norm, a_w_in, a_q_norm, a_k_norm, a_w_o)
            ys, a_kv_s, a_idx_s = dsa_sample(ys, cache_a_kv, cache_a_idx, page_table,
                                             a_norm, a_w_in, a_q_norm, a_k_norm, a_w_o)
        else:
            yp, b_cmp_p, b_slc_p, b_win_p = nsa_prompt(yp, b_norm, b_w_in, b_q_norm, b_k_norm,
                                                       b_cmp_w, b_cmp_pe, b_w_o)
            ys, b_cmp_s, b_slc_s, b_win_s = nsa_sample(ys, cache_b_cmp_kv, cache_b_slc_kv, state_b_win_kv,
                                                       page_table, b_norm, b_w_in, b_q_norm, b_k_norm,
                                                       b_cmp_w, b_cmp_pe, b_w_o)
    return (yp, ys, a_kv_p, a_idx_p, a_kv_s, a_idx_s, b_cmp_p, b_slc_p, b_win_p, b_cmp_s, b_slc_s, b_win_s)
```

```python
import functools

import numpy as np
import jax
import jax.numpy as jnp
from jax import lax
from jax.experimental import pallas as pl
from jax.experimental.pallas import tpu as pltpu

F32 = jnp.float32
BF16 = jnp.bfloat16
I32 = jnp.int32

D_MODEL = 1024
N_HEADS = 8
HEAD_DIM = 128
WIDTH = N_HEADS * HEAD_DIM
ROT_DIM = HEAD_DIM // 4
ROPE_THETA = 500000.0
EPS = 1e-6
ATTN_SCALE = HEAD_DIM ** -0.5
PAGE_SIZE = 128
A_KV_HEADS = 4
IDX_HEADS = 8
IDX_DIM = 64
IDX_ROT = IDX_DIM // 4
TOPK_MAX = 256
B_KV_HEADS = 2
GB = N_HEADS // B_KV_HEADS
CMP_BLOCK = 32
CMP_STRIDE = 16
SLC_BLOCK = 64
SLC_TOPN = 16
WINDOW = 512

LANES = 128
NEG_BIG = -1e30
KEY_NEG_INF = -2139095041
VMEM_LIMIT = 56 * 1024 * 1024

_NT = (((1,), (1,)), ((), ()))


def _cparams(n_grid_dims):
    return pltpu.CompilerParams(
        dimension_semantics=("arbitrary",) * n_grid_dims,
        vmem_limit_bytes=VMEM_LIMIT)


def _rope_tables(pos, rot_dim):
    half = rot_dim // 2
    freq = ROPE_THETA ** (-jnp.arange(half, dtype=F32) / half)
    ang = pos.astype(F32)[:, None] * freq
    cos, sin = jnp.cos(ang), jnp.sin(ang)
    t = pos.shape[0]
    one = jnp.ones((t, LANES - rot_dim), F32)
    zero = jnp.zeros((t, LANES - rot_dim), F32)
    zh = jnp.zeros((t, half), F32)
    c = jnp.concatenate([cos, cos, one], axis=1)
    s1 = jnp.concatenate([-sin, zh, zero], axis=1)
    s2 = jnp.concatenate([zh, sin, zero], axis=1)
    return jnp.stack([c, s1, s2])


def _rope(t, tab_ref, half):
    return (t * tab_ref[0] + pltpu.roll(t, LANES - half, 1) * tab_ref[1]
            + pltpu.roll(t, half, 1) * tab_ref[2])


def _head_norm(t, g):
    return t * lax.rsqrt(jnp.mean(t * t, axis=-1, keepdims=True) + EPS) * g


A_COLS = 4 * WIDTH + LANES


def _layout_w_a(w_in):
    o = 0
    wq = w_in[:, o:o + WIDTH]; o += WIDTH
    wkv = w_in[:, o:o + 2 * A_KV_HEADS * HEAD_DIM]; o += 2 * A_KV_HEADS * HEAD_DIM
    wiq = w_in[:, o:o + IDX_HEADS * IDX_DIM]; o += IDX_HEADS * IDX_DIM
    wik = w_in[:, o:o + IDX_DIM]; o += IDX_DIM
    wiw = w_in[:, o:o + IDX_HEADS]; o += IDX_HEADS
    wz = w_in[:, o:o + WIDTH]
    d = w_in.shape[0]
    wiq = jnp.pad(wiq.reshape(d, IDX_HEADS, IDX_DIM), ((0, 0), (0, 0), (0, LANES - IDX_DIM))).reshape(d, -1)
    wikw = jnp.concatenate([wik, wiw, jnp.zeros((d, LANES - IDX_DIM - IDX_HEADS), w_in.dtype)], axis=1)
    return jnp.concatenate([wq, wkv, wiq, wikw, wz], axis=1).astype(BF16)


def _proj_a_kernel(x_ref, g_ref, w_ref, qg_ref, kg_ref, rt_ref, it_ref,
                   q_out, kv_out, k_out, v_out, iq_out, ikw_out, ikp_out, z_out, xn_ref):
    x = x_ref[...]
    ms = jnp.mean(x * x, axis=-1, keepdims=True)
    xn_ref[...] = (x * lax.rsqrt(ms + EPS) * g_ref[...]).astype(BF16)

    def seg(c0):
        return jnp.dot(xn_ref[...], w_ref[:, c0:c0 + LANES], preferred_element_type=F32)

    qg = qg_ref[...]
    kg = kg_ref[...]
    for h in range(N_HEADS):
        t = _rope(_head_norm(seg(h * LANES), qg), rt_ref, ROT_DIM // 2)
        q_out[:, h * LANES:(h + 1) * LANES] = (t * ATTN_SCALE).astype(BF16)
    base = WIDTH
    for g in range(A_KV_HEADS):
        k = _rope(_head_norm(seg(base + (2 * g) * LANES), kg), rt_ref, ROT_DIM // 2)
        v = seg(base + (2 * g + 1) * LANES)
        kv_out[:, (2 * g) * LANES:(2 * g + 1) * LANES] = k
        kv_out[:, (2 * g + 1) * LANES:(2 * g + 2) * LANES] = v
        k_out[g] = k.astype(BF16)
        v_out[g] = v.astype(BF16)
    base = 2 * WIDTH
    for h in range(IDX_HEADS):
        t = _rope(seg(base + h * LANES), it_ref, IDX_ROT // 2)
        iq_out[:, h * LANES:(h + 1) * LANES] = t.astype(BF16)
    base = 3 * WIDTH
    t = _rope(seg(base), it_ref, IDX_ROT // 2)
    lane = lax.broadcasted_iota(I32, t.shape, 1)
    w_scale = (IDX_HEADS * IDX_DIM) ** -0.5
    ikw_out[...] = jnp.where(lane < IDX_DIM, t, t * w_scale)
    ikp_out[...] = jnp.where(lane < IDX_DIM, t, 0.0).astype(BF16)
    base = 3 * WIDTH + LANES
    for h in range(N_HEADS):
        z_out[:, h * LANES:(h + 1) * LANES] = seg(base + h * LANES)


def _proj_a(x, norm_g, w_a, q_g, k_g, rope_tab, idx_tab, tm):
    n = x.shape[0]
    period = rope_tab.shape[1] // tm
    row = lambda i: (i, 0)
    const = lambda i: (0, 0)
    tab = lambda i: (0, i % period, 0)
    hm = lambda i: (0, i, 0)
    return pl.pallas_call(
        _proj_a_kernel,
        grid=(n // tm,),
        in_specs=[
            pl.BlockSpec((tm, D_MODEL), row),
            pl.BlockSpec((1, D_MODEL), const),
            pl.BlockSpec((D_MODEL, A_COLS), const),
            pl.BlockSpec((1, HEAD_DIM), const),
            pl.BlockSpec((1, HEAD_DIM), const),
            pl.BlockSpec((3, tm, LANES), tab),
            pl.BlockSpec((3, tm, LANES), tab),
        ],
        out_specs=[
            pl.BlockSpec((tm, WIDTH), row),
            pl.BlockSpec((tm, 2 * A_KV_HEADS * HEAD_DIM), row),
            pl.BlockSpec((A_KV_HEADS, tm, HEAD_DIM), hm),
            pl.BlockSpec((A_KV_HEADS, tm, HEAD_DIM), hm),
            pl.BlockSpec((tm, IDX_HEADS * LANES), row),
            pl.BlockSpec((tm, LANES), row),
            pl.BlockSpec((tm, LANES), row),
            pl.BlockSpec((tm, WIDTH), row),
        ],
        out_shape=[
            jax.ShapeDtypeStruct((n, WIDTH), BF16),
            jax.ShapeDtypeStruct((n, 2 * A_KV_HEADS * HEAD_DIM), F32),
            jax.ShapeDtypeStruct((A_KV_HEADS, n, HEAD_DIM), BF16),
            jax.ShapeDtypeStruct((A_KV_HEADS, n, HEAD_DIM), BF16),
            jax.ShapeDtypeStruct((n, IDX_HEADS * LANES), BF16),
            jax.ShapeDtypeStruct((n, LANES), F32),
            jax.ShapeDtypeStruct((n, LANES), BF16),
            jax.ShapeDtypeStruct((n, WIDTH), F32),
        ],
        scratch_shapes=[pltpu.VMEM((tm, D_MODEL), BF16)],
        compiler_params=_cparams(1),
        name="proj_a",
    )(x, norm_g.reshape(1, -1), w_a, q_g.reshape(1, -1), k_g.reshape(1, -1), rope_tab, idx_tab)


def _out_proj_kernel(x_ref, o_ref, z_ref, w_ref, y_ref):
    z = z_ref[...]
    h = (o_ref[...] * (z * jax.nn.sigmoid(z))).astype(BF16)
    y_ref[...] = x_ref[...] + jnp.dot(h, w_ref[...], preferred_element_type=F32)


def _out_proj(x, o, z, w_o, tm):
    n = x.shape[0]
    row = lambda i: (i, 0)
    return pl.pallas_call(
        _out_proj_kernel,
        grid=(n // tm,),
        in_specs=[pl.BlockSpec((tm, D_MODEL), row), pl.BlockSpec((tm, WIDTH), row),
                  pl.BlockSpec((tm, WIDTH), row), pl.BlockSpec((WIDTH, D_MODEL), lambda i: (0, 0))],
        out_specs=pl.BlockSpec((tm, D_MODEL), row),
        out_shape=jax.ShapeDtypeStruct((n, D_MODEL), F32),
        compiler_params=_cparams(1),
        name="out_proj",
    )(x, o, z, w_o.astype(BF16))


def _key_to_float(t):
    bits = t ^ ((t >> 31) & 0x7FFFFFFF)
    return pltpu.bitcast(bits, F32)


def _row_count(sc_ref, n_tiles, pred):
    rows = sc_ref.shape[0]

    def body(j, c):
        c0 = pl.multiple_of(j * LANES, LANES)
        return c + jnp.where(pred(sc_ref[:, pl.ds(c0, LANES)], c0), 1, 0)

    c = lax.fori_loop(0, n_tiles, body, jnp.zeros((rows, LANES), I32))
    return jnp.broadcast_to(jnp.sum(c, axis=1, keepdims=True), (rows, LANES))


def _topk_threshold(sc_ref, n_tiles, k):
    rows = sc_ref.shape[0]

    def count_ge(t):
        thr = _key_to_float(t)
        return _row_count(sc_ref, n_tiles, lambda blk, c0: blk >= thr)

    zero = jnp.zeros((rows, LANES), I32)
    t = jnp.where(count_ge(zero) >= k, zero, jnp.full((rows, LANES), -2 ** 31, I32))

    def bit_body(it, t):
        cand = t + (jnp.int32(1) << (30 - it))
        return jnp.where(count_ge(cand) >= k, cand, t)

    t = lax.fori_loop(0, 31, bit_body, t)
    t = jnp.maximum(t, KEY_NEG_INF)
    thr = _key_to_float(t)
    n_gt = _row_count(sc_ref, n_tiles, lambda blk, c0: blk > thr)
    n_ge = _row_count(sc_ref, n_tiles, lambda blk, c0: blk >= thr)
    need = k - n_gt
    short = thr == -jnp.inf
    excess = jnp.where(short, 0, (n_ge - n_gt) - need)
    big = jnp.int32(2 ** 30)

    def tie_limit():
        def jbody(it, jl):
            cand = jl + (jnp.int32(1) << (29 - it))
            lane = lax.broadcasted_iota(I32, (rows, LANES), 1)
            cnt = _row_count(sc_ref, n_tiles,
                             lambda blk, c0: (blk == thr) & ((lane + c0) < cand))
            return jnp.where(cnt <= need, cand, jl)
        return lax.fori_loop(0, 30, jbody, zero)

    any_excess = jnp.max(excess) > 0
    jlim = lax.cond(any_excess, tie_limit, lambda: jnp.full((rows, LANES), big, I32))
    jlim = jnp.where(short, 0, jlim)
    return thr, jlim


def _selected(s, thr, jlim, col):
    return (s > thr) | ((s == thr) & (col < jlim))


def _dsa_prompt_kernel(iq_ref, ikw_ref, ik_ref, q_ref, k_ref, v_ref, o_ref,
                       sc_ref, m_ref, l_ref, acc_ref, *, tq, tk, n_keep):
    i = pl.program_id(1)
    q0 = i * tq
    nkb = lax.div(q0 + tq + tk - 1, tk)
    tiles_per_kb = tk // LANES

    def score_body(j, carry):
        k0 = pl.multiple_of(j * tk, tk)
        ikb = ik_ref[pl.ds(k0, tk), :]
        acc = jnp.zeros((tq, tk), F32)
        for h in range(IDX_HEADS):
            lg = lax.dot_general(iq_ref[:, h * LANES:(h + 1) * LANES], ikb, _NT,
                                 preferred_element_type=F32)
            w = ikw_ref[:, IDX_DIM + h:IDX_DIM + h + 1]
            acc = acc + w * jnp.maximum(lg, 0.0)
        kpos = k0 + lax.broadcasted_iota(I32, (tq, tk), 1)
        tpos = q0 + lax.broadcasted_iota(I32, (tq, tk), 0)
        sc_ref[:, pl.ds(k0, tk)] = jnp.where(kpos <= tpos, acc, -jnp.inf)
        return carry

    lax.fori_loop(0, nkb, score_body, 0)

    thr, jlim = _topk_threshold(sc_ref, nkb * tiles_per_kb, n_keep)

    m_ref[...] = jnp.full(m_ref.shape, NEG_BIG, F32)
    l_ref[...] = jnp.zeros(l_ref.shape, F32)
    acc_ref[...] = jnp.zeros(acc_ref.shape, F32)
    lane = lax.broadcasted_iota(I32, (tq, LANES), 1)

    def attn_body(j, carry):
        k0 = pl.multiple_of(j * tk, tk)
        valid = jnp.concatenate(
            [jnp.where(_selected(sc_ref[:, pl.ds(k0 + t * LANES, LANES)], thr, jlim,
                                 lane + (k0 + t * LANES)), 1.0, 0.0)
             for t in range(tiles_per_kb)], axis=1) > 0.5
        for h in range(N_HEADS):
            g = h // (N_HEADS // A_KV_HEADS)
            s = lax.dot_general(q_ref[:, h * LANES:(h + 1) * LANES], k_ref[g, pl.ds(k0, tk), :], _NT,
                                preferred_element_type=F32)
            s = jnp.where(valid, s, NEG_BIG)
            m_prev = m_ref[h]
            m_new = jnp.maximum(m_prev, jnp.max(s, axis=1, keepdims=True))
            alpha = jnp.exp(m_prev - m_new)
            p = jnp.where(valid, jnp.exp(s - m_new[:, :1]), 0.0)
            l_ref[h] = alpha * l_ref[h] + jnp.sum(p, axis=1, keepdims=True)
            acc_ref[h] = alpha * acc_ref[h] + jnp.dot(p.astype(BF16), v_ref[g, pl.ds(k0, tk), :],
                                                      preferred_element_type=F32)
            m_ref[h] = m_new
        return carry

    lax.fori_loop(0, nkb, attn_body, 0)
    for h in range(N_HEADS):
        o_ref[:, h * LANES:(h + 1) * LANES] = acc_ref[h] / l_ref[h]


def _dsa_prompt(iq, ikw, ikp, q, k_hm, v_hm, batch, seq, n_keep, tq=256, tk=512):
    tq = min(tq, seq)
    tk = min(tk, seq)
    nq = seq // tq
    n = batch * seq
    row = lambda b, i: (b * nq + i, 0)
    kern = functools.partial(_dsa_prompt_kernel, tq=tq, tk=tk, n_keep=n_keep)
    once = pl.Buffered(1)
    return pl.pallas_call(
        kern,
        grid=(batch, nq),
        in_specs=[
            pl.BlockSpec((tq, IDX_HEADS * LANES), row),
            pl.BlockSpec((tq, LANES), row),
            pl.BlockSpec((seq, LANES), lambda b, i: (b, 0), pipeline_mode=once),
            pl.BlockSpec((tq, WIDTH), row),
            pl.BlockSpec((A_KV_HEADS, seq, HEAD_DIM), lambda b, i: (0, b, 0), pipeline_mode=once),
            pl.BlockSpec((A_KV_HEADS, seq, HEAD_DIM), lambda b, i: (0, b, 0), pipeline_mode=once),
        ],
        out_specs=pl.BlockSpec((tq, WIDTH), row),
        out_shape=jax.ShapeDtypeStruct((n, WIDTH), F32),
        scratch_shapes=[
            pltpu.VMEM((tq, seq), F32),
            pltpu.VMEM((N_HEADS, tq, LANES), F32),
            pltpu.VMEM((N_HEADS, tq, LANES), F32),
            pltpu.VMEM((N_HEADS, tq, HEAD_DIM), F32),
        ],
        compiler_params=_cparams(2),
        name="dsa_prompt",
    )(iq, ikw, ikp, q, k_hm, v_hm)


def _dsa_prompt_layer(x, norm_g, w_in, q_g, k_g, w_o):
    b, s, _ = x.shape
    n = b * s
    pos = jnp.arange(s)
    tm = min(256, s)
    xf = x.reshape(n, D_MODEL)
    q, kv, k_hm, v_hm, iq, ikw, ikp, z = _proj_a(
        xf, norm_g, _layout_w_a(w_in), q_g, k_g, _rope_tables(pos, ROT_DIM), _rope_tables(pos, IDX_ROT), tm)
    o = _dsa_prompt(iq, ikw, ikp, q, k_hm, v_hm, b, s, min(TOPK_MAX, s // 4))
    y = _out_proj(xf, o, z, w_o, tm)
    return (y.reshape(b, s, D_MODEL), kv.reshape(b, s, A_KV_HEADS, 2, HEAD_DIM),
            ikw[:, :IDX_DIM].reshape(b, s, IDX_DIM))


B_KVW = 2 * B_KV_HEADS * HEAD_DIM
B_COLS = 2 * WIDTH + 3 * B_KVW + LANES


def _layout_w_b(w_in):
    o = WIDTH + 3 * B_KVW
    d = w_in.shape[0]
    wg = jnp.concatenate([w_in[:, o:o + 3 * N_HEADS], jnp.zeros((d, LANES - 3 * N_HEADS), w_in.dtype)], axis=1)
    return jnp.concatenate([w_in[:, :o], wg, w_in[:, o + 3 * N_HEADS:]], axis=1).astype(BF16)


def _proj_b_kernel(x_ref, g_ref, w_ref, qg_ref, kg_ref, rt_ref,
                   q_out, kvc_out, kvs_out, kvw_out, ks_out, vs_out, kw_out, vw_out, gate_out, z_out, xn_ref):
    x = x_ref[...]
    ms = jnp.mean(x * x, axis=-1, keepdims=True)
    xn_ref[...] = (x * lax.rsqrt(ms + EPS) * g_ref[...]).astype(BF16)

    def seg(c0):
        return jnp.dot(xn_ref[...], w_ref[:, c0:c0 + LANES], preferred_element_type=F32)

    qg = qg_ref[...]
    for h in range(N_HEADS):
        t = _rope(_head_norm(seg(h * LANES), qg), rt_ref, ROT_DIM // 2)
        q_out[:, h * LANES:(h + 1) * LANES] = (t * ATTN_SCALE).astype(BF16)
    branch_out = ((kvc_out, None, None), (kvs_out, ks_out, vs_out), (kvw_out, kw_out, vw_out))
    for br, (kv_out, k_out, v_out) in enumerate(branch_out):
        base = WIDTH + br * B_KVW
        kg = kg_ref[br:br + 1, :]
        for g in range(B_KV_HEADS):
            k = _rope(_head_norm(seg(base + (2 * g) * LANES), kg), rt_ref, ROT_DIM // 2)
            v = seg(base + (2 * g + 1) * LANES)
            kv_out[:, (2 * g) * LANES:(2 * g + 1) * LANES] = k
            kv_out[:, (2 * g + 1) * LANES:(2 * g + 2) * LANES] = v
            if k_out is not None:
                k_out[g] = k.astype(BF16)
                v_out[g] = v.astype(BF16)
    base = WIDTH + 3 * B_KVW
    gate_out[...] = jax.nn.sigmoid(seg(base))
    base += LANES
    for h in range(N_HEADS):
        z_out[:, h * LANES:(h + 1) * LANES] = seg(base + h * LANES)


def _proj_b(x, norm_g, w_b, q_g, k_g, rope_tab, tm):
    n = x.shape[0]
    period = rope_tab.shape[1] // tm
    row = lambda i: (i, 0)
    const = lambda i: (0, 0)
    hm = lambda i: (0, i, 0)
    kv_spec = pl.BlockSpec((tm, B_KVW), row)
    hm_spec = pl.BlockSpec((B_KV_HEADS, tm, HEAD_DIM), hm)
    kv_shape = jax.ShapeDtypeStruct((n, B_KVW), F32)
    hm_shape = jax.ShapeDtypeStruct((B_KV_HEADS, n, HEAD_DIM), BF16)
    return pl.pallas_call(
        _proj_b_kernel,
        grid=(n // tm,),
        in_specs=[
            pl.BlockSpec((tm, D_MODEL), row),
            pl.BlockSpec((1, D_MODEL), const),
            pl.BlockSpec((D_MODEL, B_COLS), const),
            pl.BlockSpec((1, HEAD_DIM), const),
            pl.BlockSpec((3, HEAD_DIM), const),
            pl.BlockSpec((3, tm, LANES), lambda i: (0, i % period, 0)),
        ],
        out_specs=[pl.BlockSpec((tm, WIDTH), row), kv_spec, kv_spec, kv_spec,
                   hm_spec, hm_spec, hm_spec, hm_spec,
                   pl.BlockSpec((tm, LANES), row), pl.BlockSpec((tm, WIDTH), row)],
        out_shape=[jax.ShapeDtypeStruct((n, WIDTH), BF16), kv_shape, kv_shape, kv_shape,
                   hm_shape, hm_shape, hm_shape, hm_shape,
                   jax.ShapeDtypeStruct((n, LANES), F32), jax.ShapeDtypeStruct((n, WIDTH), F32)],
        scratch_shapes=[pltpu.VMEM((tm, D_MODEL), BF16)],
        compiler_params=_cparams(1),
        name="proj_b",
    )(x, norm_g.reshape(1, -1), w_b, q_g.reshape(1, -1), k_g, rope_tab)


CHUNK_COLS = CMP_STRIDE * B_KVW


def _layout_cmp(w_cmp, pe_cmp):
    r = CMP_BLOCK // CMP_STRIDE
    eye_g = jnp.eye(B_KV_HEADS, dtype=w_cmp.dtype)
    eye_c = jnp.eye(2, dtype=w_cmp.dtype)
    wj = w_cmp.reshape(r, CMP_STRIDE, 2, HEAD_DIM, HEAD_DIM)
    w = jnp.einsum('jlcde,gh,ck->jlgcdhke', wj, eye_g, eye_c).reshape(r, CHUNK_COLS, B_KVW)
    pe = jnp.broadcast_to(pe_cmp.reshape(r, CMP_STRIDE, 1, 2, HEAD_DIM),
                          (r, CMP_STRIDE, B_KV_HEADS, 2, HEAD_DIM)).reshape(r, 1, CHUNK_COLS)
    return w.astype(BF16), pe


def _cmp_mm_kernel(x_ref, pe_ref, w_ref, p0_ref, p1_ref):
    x = x_ref[...]
    p0_ref[...] = jnp.dot((x + pe_ref[0]).astype(BF16), w_ref[0], preferred_element_type=F32)
    p1_ref[...] = jnp.dot((x + pe_ref[1]).astype(BF16), w_ref[1], preferred_element_type=F32)


def _cmp_mm(x, w, pe, tr):
    r = x.shape[0]
    row = lambda i: (i, 0)
    once = pl.Buffered(1)
    return pl.pallas_call(
        _cmp_mm_kernel,
        grid=(r // tr,),
        in_specs=[pl.BlockSpec((tr, CHUNK_COLS), row),
                  pl.BlockSpec((2, 1, CHUNK_COLS), lambda i: (0, 0, 0)),
                  pl.BlockSpec((2, CHUNK_COLS, B_KVW), lambda i: (0, 0, 0), pipeline_mode=once)],
        out_specs=[pl.BlockSpec((tr, B_KVW), row), pl.BlockSpec((tr, B_KVW), row)],
        out_shape=[jax.ShapeDtypeStruct((r, B_KVW), F32), jax.ShapeDtypeStruct((r, B_KVW), F32)],
        compiler_params=_cparams(1),
        name="cmp_mm",
    )(x, pe, w)


def _overlap_matrix(n_rows, n_cmp, n_cols):
    i = np.arange(n_rows)[:, None]
    j = np.arange(n_cols)[None, :]
    lo = np.maximum(i * CMP_STRIDE, j * SLC_BLOCK)
    hi = np.minimum(i * CMP_STRIDE + CMP_BLOCK, (j + 1) * SLC_BLOCK)
    ov = np.maximum(hi - lo, 0) / CMP_STRIDE
    ov = np.where(i < n_cmp, ov, 0.0)
    return jnp.asarray(ov, dtype=BF16)


def _dot_f32(a, b_bf16):
    hi = a.astype(BF16)
    r1 = a - hi.astype(F32)
    mid = r1.astype(BF16)
    lo = (r1 - mid.astype(F32)).astype(BF16)
    d = lambda p: jnp.dot(p, b_bf16, preferred_element_type=F32)
    return d(hi) + d(mid) + d(lo)


def _flash_step(q, k, v, valid, m_ref, l_ref, acc_ref, h):
    s = lax.dot_general(q, k, _NT, preferred_element_type=F32)
    s = jnp.where(valid, s, NEG_BIG)
    m_prev = m_ref[h]
    m_new = jnp.maximum(m_prev, jnp.max(s, axis=1, keepdims=True))
    alpha = jnp.exp(m_prev - m_new)
    p = jnp.where(valid, jnp.exp(s - m_new[:, :1]), 0.0)
    l_ref[h] = alpha * l_ref[h] + jnp.sum(p, axis=1, keepdims=True)
    acc_ref[h] = alpha * acc_ref[h] + jnp.dot(p.astype(BF16), v, preferred_element_type=F32)
    m_ref[h] = m_new


def _flash_init(m_ref, l_ref, acc_ref):
    m_ref[...] = jnp.full(m_ref.shape, NEG_BIG, F32)
    l_ref[...] = jnp.zeros(l_ref.shape, F32)
    acc_ref[...] = jnp.zeros(acc_ref.shape, F32)


def _nsa_prompt_kernel(q_ref, gate_ref, p0_ref, p1_ref, ov_ref, ks_ref, vs_ref, kw_ref, vw_ref, o_ref,
                       kc_ref, vc_ref, imp_ref, oc_ref, os_ref, m_ref, l_ref, acc_ref,
                       *, tq, tk, n_cmp, n_sel):
    i = pl.program_id(1)
    q0 = i * tq
    nc = p0_ref.shape[0]

    @pl.when(i == 0)
    def _():
        kc = p0_ref[...] + pltpu.roll(p1_ref[...], nc - 1, 0)
        for g in range(B_KV_HEADS):
            kc_ref[g] = kc[:, (2 * g) * LANES:(2 * g + 1) * LANES].astype(BF16)
            vc_ref[g] = kc[:, (2 * g + 1) * LANES:(2 * g + 2) * LANES].astype(BF16)

    tpos_c = q0 + lax.broadcasted_iota(I32, (tq, nc), 0)
    nidx = lax.broadcasted_iota(I32, (tq, nc), 1)
    cvalid = jnp.where(nidx < n_cmp, nidx * CMP_STRIDE + (CMP_BLOCK - 1), 2 ** 30) <= tpos_c
    tpos = q0 + lax.broadcasted_iota(I32, (tq, LANES), 0)
    jblk = lax.broadcasted_iota(I32, (tq, LANES), 1)
    cur = tpos >> 6
    forced = (jblk == 0) | (jblk == cur) | (jblk == cur - 1)
    admissible = jblk * SLC_BLOCK <= tpos
    sel = []
    for g in range(B_KV_HEADS):
        pg = jnp.zeros((tq, nc), F32)
        for r in range(GB):
            h = g * GB + r
            s = lax.dot_general(q_ref[:, h * LANES:(h + 1) * LANES], kc_ref[g], _NT,
                                preferred_element_type=F32)
            s = jnp.where(cvalid, s, NEG_BIG)
            e = jnp.where(cvalid, jnp.exp(s - jnp.max(s, axis=1, keepdims=True)), 0.0)
            l = jnp.sum(e, axis=1, keepdims=True)
            p = e / jnp.where(l > 0.0, l, 1.0)
            oc_ref[h] = jnp.dot(p.astype(BF16), vc_ref[g], preferred_element_type=F32)
            pg = pg + p
        imp = _dot_f32(pg, ov_ref[...])
        imp_ref[...] = jnp.where(forced, jnp.inf, jnp.where(admissible, imp, -jnp.inf))
        thr, jlim = _topk_threshold(imp_ref, 1, n_sel)
        sel.append(jnp.where(_selected(imp_ref[...], thr, jlim, jblk), 1.0, 0.0).astype(BF16))

    nkb = lax.div(q0 + tq + tk - 1, tk)
    _flash_init(m_ref, l_ref, acc_ref)
    blk_row = lax.broadcasted_iota(I32, (LANES, tk), 0)
    key_col = lax.broadcasted_iota(I32, (LANES, tk), 1)
    tpos_k = q0 + lax.broadcasted_iota(I32, (tq, tk), 0)
    kcol = lax.broadcasted_iota(I32, (tq, tk), 1)

    def slc_body(j, carry):
        k0 = pl.multiple_of(j * tk, tk)
        expand = jnp.where(((key_col + k0) >> 6) == blk_row, 1.0, 0.0).astype(BF16)
        causal = (kcol + k0) <= tpos_k
        for g in range(B_KV_HEADS):
            hit = jnp.dot(sel[g], expand, preferred_element_type=F32)
            valid = jnp.where(causal, hit, 0.0) > 0.5
            for r in range(GB):
                h = g * GB + r
                _flash_step(q_ref[:, h * LANES:(h + 1) * LANES], ks_ref[g, pl.ds(k0, tk), :],
                            vs_ref[g, pl.ds(k0, tk), :], valid, m_ref, l_ref, acc_ref, h)
        return carry

    lax.fori_loop(0, nkb, slc_body, 0)
    for h in range(N_HEADS):
        os_ref[h] = acc_ref[h] / l_ref[h]

    _flash_init(m_ref, l_ref, acc_ref)
    tpos_w = q0 + lax.broadcasted_iota(I32, (tq, tq), 0)
    wcol = lax.broadcasted_iota(I32, (tq, tq), 1)

    def win_body(j, carry):
        k0 = pl.multiple_of(j * tq, tq)
        kpos = wcol + k0
        valid = jnp.where(kpos <= tpos_w, kpos, -2 ** 30) > tpos_w - WINDOW
        for h in range(N_HEADS):
            g = h // GB
            _flash_step(q_ref[:, h * LANES:(h + 1) * LANES], kw_ref[g, pl.ds(k0, tq), :],
                        vw_ref[g, pl.ds(k0, tq), :], valid, m_ref, l_ref, acc_ref, h)
        return carry

    lax.fori_loop(jnp.maximum(i - WINDOW // tq, 0), i + 1, win_body, 0)
    for h in range(N_HEADS):
        ow = acc_ref[h] / l_ref[h]
        o_ref[:, h * LANES:(h + 1) * LANES] = (gate_ref[:, 3 * h:3 * h + 1] * oc_ref[h]
                                               + gate_ref[:, 3 * h + 1:3 * h + 2] * os_ref[h]
                                               + gate_ref[:, 3 * h + 2:3 * h + 3] * ow)


def _nsa_prompt(q, gates, p0, p1, ks, vs, kw, vw, batch, seq, tq=256, tk=512):
    tq = min(tq, seq)
    tk = min(tk, seq)
    nq = seq // tq
    n = batch * seq
    nc = seq // CMP_STRIDE
    n_cmp = nc - CMP_BLOCK // CMP_STRIDE + 1
    n_slc = seq // SLC_BLOCK
    assert n_slc <= LANES and WINDOW % tq == 0
    ov = _overlap_matrix(nc, n_cmp, LANES)
    row = lambda b, i: (b * nq + i, 0)
    per_b = lambda b, i: (b, 0)
    hm_b = lambda b, i: (0, b, 0)
    once = pl.Buffered(1)
    kern = functools.partial(_nsa_prompt_kernel, tq=tq, tk=tk, n_cmp=n_cmp, n_sel=min(SLC_TOPN, n_slc))
    hm_spec = pl.BlockSpec((B_KV_HEADS, seq, HEAD_DIM), hm_b, pipeline_mode=once)
    return pl.pallas_call(
        kern,
        grid=(batch, nq),
        in_specs=[
            pl.BlockSpec((tq, WIDTH), row),
            pl.BlockSpec((tq, LANES), row),
            pl.BlockSpec((nc, B_KVW), per_b, pipeline_mode=once),
            pl.BlockSpec((nc, B_KVW), per_b, pipeline_mode=once),
            pl.BlockSpec((nc, LANES), lambda b, i: (0, 0), pipeline_mode=once),
            hm_spec, hm_spec, hm_spec, hm_spec,
        ],
        out_specs=pl.BlockSpec((tq, WIDTH), row),
        out_shape=jax.ShapeDtypeStruct((n, WIDTH), F32),
        scratch_shapes=[
            pltpu.VMEM((B_KV_HEADS, nc, HEAD_DIM), BF16),
            pltpu.VMEM((B_KV_HEADS, nc, HEAD_DIM), BF16),
            pltpu.VMEM((tq, LANES), F32),
            pltpu.VMEM((N_HEADS, tq, HEAD_DIM), F32),
            pltpu.VMEM((N_HEADS, tq, HEAD_DIM), F32),
            pltpu.VMEM((N_HEADS, tq, LANES), F32),
            pltpu.VMEM((N_HEADS, tq, LANES), F32),
            pltpu.VMEM((N_HEADS, tq, HEAD_DIM), F32),
        ],
        compiler_params=_cparams(2),
        name="nsa_prompt",
    )(q, gates, p0, p1, ov, ks, vs, kw, vw)


def _nsa_prompt_layer(x, norm_g, w_in, q_g, k_g, w_cmp, pe_cmp, w_o):
    b, s, _ = x.shape
    n = b * s
    tm = min(256, s)
    xf = x.reshape(n, D_MODEL)
    q, kvc, kvs, kvw, ks, vs, kw, vw, gates, z = _proj_b(
        xf, norm_g, _layout_w_b(w_in), q_g, k_g, _rope_tables(jnp.arange(s), ROT_DIM), tm)
    wc, pe = _layout_cmp(w_cmp, pe_cmp)
    nc = s // CMP_STRIDE
    p0, p1 = _cmp_mm(kvc.reshape(b * nc, CHUNK_COLS), wc, pe, min(128, nc))
    o = _nsa_prompt(q, gates, p0, p1, ks, vs, kw, vw, b, s)
    y = _out_proj(xf, o, z, w_o, tm)
    shp = (b, s, B_KV_HEADS, 2, HEAD_DIM)
    wlen = min(WINDOW, s)
    return (y.reshape(b, s, D_MODEL), kvc.reshape(shp), kvs.reshape(shp),
            kvw.reshape(shp)[:, s - wlen:])


def _page_specs(block, pages_per_step):
    return [pl.BlockSpec((None,) + block,
                         lambda b, j, pt, k=k: (pt[b, j * pages_per_step + k],) + (0,) * len(block))
            for k in range(pages_per_step)]


def _row_token(shape):
    return lax.broadcasted_iota(I32, shape, 0) & 3


def _dsa_sample_score_kernel(pt_ref, iq_ref, iw_ref, ikn_ref, *rest, pps, past_len, n_keep):
    pages = rest[:pps]
    sc_ref, thr_ref, jlim_ref = rest[pps:]
    j = pl.program_id(1)
    iq = iq_ref[:, :IDX_DIM]

    def score(keys_bf16):
        lg = lax.dot_general(iq, keys_bf16, _NT, preferred_element_type=F32)
        acc = jnp.zeros((8, LANES), F32)
        for h in range(IDX_HEADS):
            acc = acc + iw_ref[h * 8:(h + 1) * 8, :] * jnp.maximum(lg[h * 8:(h + 1) * 8, :], 0.0)
        return acc

    for k in range(pps):
        c0 = pl.multiple_of((j * pps + k) * PAGE_SIZE, PAGE_SIZE)
        sc_ref[:, pl.ds(c0, PAGE_SIZE)] = score(pages[k][...].astype(BF16))

    @pl.when(j == pl.num_programs(1) - 1)
    def _():
        s_new = score(ikn_ref[:, :IDX_DIM])
        col = lax.broadcasted_iota(I32, (8, LANES), 1)
        sc_ref[:, past_len:past_len + LANES] = jnp.where(col <= _row_token((8, LANES)), s_new, -jnp.inf)
        thr, jlim = _topk_threshold(sc_ref, sc_ref.shape[1] // LANES, n_keep)
        thr_ref[...] = thr
        jlim_ref[...] = jlim


def _dsa_sample_scores(page_table, cache_idx, iq_s, iw_s, ik_new, n_keep, pps=8):
    nb, n_pages = page_table.shape
    past_len = n_pages * PAGE_SIZE
    scw = past_len + LANES
    per_b = lambda b, j, pt: (b, 0, 0)
    kern = functools.partial(_dsa_sample_score_kernel, pps=pps, past_len=past_len, n_keep=n_keep)
    grid_spec = pltpu.PrefetchScalarGridSpec(
        num_scalar_prefetch=1,
        grid=(nb, n_pages // pps),
        in_specs=[pl.BlockSpec((None, 64, LANES), per_b),
                  pl.BlockSpec((None, 64, LANES), per_b),
                  pl.BlockSpec((None, LANES, LANES), per_b)] + _page_specs((PAGE_SIZE, IDX_DIM), pps),
        out_specs=[pl.BlockSpec((None, 8, scw), per_b),
                   pl.BlockSpec((None, 8, LANES), per_b),
                   pl.BlockSpec((None, 8, LANES), per_b)],
    )
    return pl.pallas_call(
        kern,
        grid_spec=grid_spec,
        out_shape=[jax.ShapeDtypeStruct((nb, 8, scw), F32),
                   jax.ShapeDtypeStruct((nb, 8, LANES), F32),
                   jax.ShapeDtypeStruct((nb, 8, LANES), I32)],
        compiler_params=_cparams(2),
        name="dsa_sample_scores",
    )(page_table, iq_s, iw_s, ik_new, *([cache_idx] * pps))


def _dsa_sample_attn_kernel(pt_ref, q_ref, sc_ref, thr_ref, jlim_ref, kn_ref, vn_ref, *rest, pps, past_len):
    pages = rest[:pps]
    o_ref, m_ref, l_ref, acc_ref = rest[pps:]
    j = pl.program_id(1)
    thr = thr_ref[...]
    jlim = jlim_ref[...]
    lane = lax.broadcasted_iota(I32, (8, LANES), 1)

    @pl.when(j == 0)
    def _():
        _flash_init(m_ref, l_ref, acc_ref)

    def valid_at(c0):
        return _selected(sc_ref[:, pl.ds(c0, LANES)], thr, jlim, lane + c0)

    for k in range(pps):
        c0 = pl.multiple_of((j * pps + k) * PAGE_SIZE, PAGE_SIZE)
        valid = valid_at(c0)
        for g in range(A_KV_HEADS):
            kk = pages[k][:, (2 * g) * LANES:(2 * g + 1) * LANES].astype(BF16)
            vv = pages[k][:, (2 * g + 1) * LANES:(2 * g + 2) * LANES].astype(BF16)
            _flash_step(q_ref[g], kk, vv, valid, m_ref, l_ref, acc_ref, g)

    @pl.when(j == pl.num_programs(1) - 1)
    def _():
        valid = valid_at(past_len)
        for g in range(A_KV_HEADS):
            _flash_step(q_ref[g], kn_ref[g], vn_ref[g], valid, m_ref, l_ref, acc_ref, g)
            o_ref[g] = acc_ref[g] / l_ref[g]


def _dsa_sample_attn(page_table, cache_kv, q_s, scores, thr, jlim, k_new, v_new, pps=4):
    nb, n_pages = page_table.shape
    past_len = n_pages * PAGE_SIZE
    scw = scores.shape[-1]
    per_b3 = lambda b, j, pt: (b, 0, 0)
    per_b4 = lambda b, j, pt: (b, 0, 0, 0)
    kv_cols = 2 * A_KV_HEADS * HEAD_DIM
    kern = functools.partial(_dsa_sample_attn_kernel, pps=pps, past_len=past_len)
    grid_spec = pltpu.PrefetchScalarGridSpec(
        num_scalar_prefetch=1,
        grid=(nb, n_pages // pps),
        in_specs=[pl.BlockSpec((None, A_KV_HEADS, 8, HEAD_DIM), per_b4),
                  pl.BlockSpec((None, 8, scw), per_b3),
                  pl.BlockSpec((None, 8, LANES), per_b3),
                  pl.BlockSpec((None, 8, LANES), per_b3),
                  pl.BlockSpec((None, A_KV_HEADS, LANES, HEAD_DIM), per_b4),
                  pl.BlockSpec((None, A_KV_HEADS, LANES, HEAD_DIM), per_b4)]
        + _page_specs((PAGE_SIZE, kv_cols), pps),
        out_specs=pl.BlockSpec((None, A_KV_HEADS, 8, HEAD_DIM), per_b4),
        scratch_shapes=[pltpu.VMEM((A_KV_HEADS, 8, LANES), F32),
                        pltpu.VMEM((A_KV_HEADS, 8, LANES), F32),
                        pltpu.VMEM((A_KV_HEADS, 8, HEAD_DIM), F32)],
    )
    return pl.pallas_call(
        kern,
        grid_spec=grid_spec,
        out_shape=jax.ShapeDtypeStruct((nb, A_KV_HEADS, 8, HEAD_DIM), F32),
        compiler_params=_cparams(2),
        name="dsa_sample_attn",
    )(page_table, q_s, scores, thr, jlim, k_new, v_new,
      *([cache_kv.reshape(cache_kv.shape[0], PAGE_SIZE, kv_cols)] * pps))


def _pad_keys(a, axis):
    pad = [(0, 0)] * a.ndim
    pad[axis] = (0, LANES - a.shape[axis])
    return jnp.pad(a, pad)


def _dsa_sample_layer(x, cache_kv, cache_idx, page_table, norm_g, w_in, q_g, k_g, w_o):
    nb, nt, _ = x.shape
    assert nt == 4
    n = nb * nt
    past_len = page_table.shape[1] * PAGE_SIZE
    pos = jnp.tile(past_len + jnp.arange(nt), nb)
    xf = x.reshape(n, D_MODEL)
    q, kv, k_hm, v_hm, iq, ikw, ikp, z = _proj_a(
        xf, norm_g, _layout_w_a(w_in), q_g, k_g, _rope_tables(pos, ROT_DIM), _rope_tables(pos, IDX_ROT), n)
    ga = N_HEADS // A_KV_HEADS
    iq_s = iq.reshape(nb, nt, IDX_HEADS, LANES).transpose(0, 2, 1, 3)
    iq_s = jnp.concatenate([iq_s, iq_s], axis=2).reshape(nb, 64, LANES)
    iw = ikw[:, IDX_DIM:IDX_DIM + IDX_HEADS].reshape(nb, nt, IDX_HEADS).transpose(0, 2, 1)
    iw_s = jnp.broadcast_to(jnp.concatenate([iw, iw], axis=2).reshape(nb, 64, 1), (nb, 64, LANES))
    ik_new = _pad_keys(ikp.reshape(nb, nt, LANES), 1)
    q_s = q.reshape(nb, nt, A_KV_HEADS, ga, HEAD_DIM).transpose(0, 2, 3, 1, 4).reshape(nb, A_KV_HEADS, 8, HEAD_DIM)
    k_new = _pad_keys(k_hm.reshape(A_KV_HEADS, nb, nt, HEAD_DIM).transpose(1, 0, 2, 3), 2)
    v_new = _pad_keys(v_hm.reshape(A_KV_HEADS, nb, nt, HEAD_DIM).transpose(1, 0, 2, 3), 2)
    n_keep = min(TOPK_MAX, (past_len + nt) // 4)
    scores, thr, jlim = _dsa_sample_scores(page_table, cache_idx, iq_s, iw_s, ik_new, n_keep)
    o_s = _dsa_sample_attn(page_table, cache_kv, q_s, scores, thr, jlim, k_new, v_new)
    o = o_s.reshape(nb, A_KV_HEADS, ga, nt, HEAD_DIM).transpose(0, 3, 1, 2, 4).reshape(n, WIDTH)
    y = _out_proj(xf, o, z, w_o, n)
    return (y.reshape(nb, nt, D_MODEL), kv.reshape(nb, nt, A_KV_HEADS, 2, HEAD_DIM),
            ikw[:, :IDX_DIM].reshape(nb, nt, IDX_DIM))


CHUNKS_PER_PAGE = PAGE_SIZE // CMP_STRIDE


def _cmp_paged_kernel(pt_ref, xn_ref, pe_ref, w_ref, *rest, pps, k_chunk):
    pages = rest[:pps]
    p0_ref, p1_ref, pn_ref, xs_ref = rest[pps:]
    for k in range(pps):
        xs_ref[k * CHUNKS_PER_PAGE:(k + 1) * CHUNKS_PER_PAGE, :] = pages[k][...]
    rows = xs_ref.shape[0]
    acc0 = jnp.zeros((rows, B_KVW), F32)
    acc1 = jnp.zeros((rows, B_KVW), F32)
    accn = jnp.zeros((8, B_KVW), F32)
    for c in range(0, CHUNK_COLS, k_chunk):
        x = xs_ref[:, c:c + k_chunk]
        acc0 = acc0 + jnp.dot((x + pe_ref[0, :, c:c + k_chunk]).astype(BF16), w_ref[0, c:c + k_chunk, :],
                              preferred_element_type=F32)
        acc1 = acc1 + jnp.dot((x + pe_ref[1, :, c:c + k_chunk]).astype(BF16), w_ref[1, c:c + k_chunk, :],
                              preferred_element_type=F32)
        accn = accn + jnp.dot((xn_ref[:, c:c + k_chunk] + pe_ref[1, :, c:c + k_chunk]).astype(BF16),
                              w_ref[1, c:c + k_chunk, :], preferred_element_type=F32)
    p0_ref[...] = acc0
    p1_ref[...] = acc1
    pn_ref[...] = accn


def _cmp_paged(page_table, cache_cmp, x_new, w, pe, pps=32, k_chunk=1024):
    nb, n_pages = page_table.shape
    pps = min(pps, n_pages)
    steps = n_pages // pps
    rows = pps * CHUNKS_PER_PAGE
    kern = functools.partial(_cmp_paged_kernel, pps=pps, k_chunk=k_chunk)
    once = pl.Buffered(1)
    grid_spec = pltpu.PrefetchScalarGridSpec(
        num_scalar_prefetch=1,
        grid=(nb, steps),
        in_specs=[pl.BlockSpec((None, 8, CHUNK_COLS), lambda b, j, pt: (b, 0, 0)),
                  pl.BlockSpec((2, 1, CHUNK_COLS), lambda b, j, pt: (0, 0, 0)),
                  pl.BlockSpec((2, CHUNK_COLS, B_KVW), lambda b, j, pt: (0, 0, 0), pipeline_mode=once)]
        + _page_specs((CHUNKS_PER_PAGE, CHUNK_COLS), pps),
        out_specs=[pl.BlockSpec((rows, B_KVW), lambda b, j, pt: (b * steps + j, 0)),
                   pl.BlockSpec((rows, B_KVW), lambda b, j, pt: (b * steps + j, 0)),
                   pl.BlockSpec((None, 8, B_KVW), lambda b, j, pt: (b, 0, 0))],
        scratch_shapes=[pltpu.VMEM((rows, CHUNK_COLS), F32)],
    )
    n_chunks = n_pages * CHUNKS_PER_PAGE
    return pl.pallas_call(
        kern,
        grid_spec=grid_spec,
        out_shape=[jax.ShapeDtypeStruct((nb * n_chunks, B_KVW), F32),
                   jax.ShapeDtypeStruct((nb * n_chunks, B_KVW), F32),
                   jax.ShapeDtypeStruct((nb, 8, B_KVW), F32)],
        compiler_params=_cparams(2),
        name="cmp_paged",
    )(page_table, x_new, pe, w,
      *([cache_cmp.reshape(cache_cmp.shape[0], CHUNKS_PER_PAGE, CHUNK_COLS)] * pps))


N_TILES_B = B_KV_HEADS * 2


def _nsa_sample_kernel(pt_ref, q_ref, gate_ref, p0_ref, p1_ref, pn_ref, ov_ref, ksn_ref, vsn_ref,
                       win_ref, kwn_ref, vwn_ref, *rest, pps, past_len, n_cmp, n_sel):
    pages = rest[:pps]
    o_ref, kc_ref, vc_ref, imp_ref, mask_ref, oc_ref, ow_ref, m_ref, l_ref, acc_ref = rest[pps:]
    j = pl.program_id(1)
    nc = p0_ref.shape[0]
    tok = _row_token((8, LANES))
    tpos = past_len + tok
    lane = lax.broadcasted_iota(I32, (8, LANES), 1)

    @pl.when(j == 0)
    def _():
        rown = lax.broadcasted_iota(I32, (nc, B_KVW), 0)
        kc = p0_ref[...] + jnp.where(rown == nc - 1, jnp.broadcast_to(pn_ref[0:1, :], (nc, B_KVW)),
                                     pltpu.roll(p1_ref[...], nc - 1, 0))
        for g in range(B_KV_HEADS):
            kc_ref[g] = kc[:, (2 * g) * LANES:(2 * g + 1) * LANES].astype(BF16)
            vc_ref[g] = kc[:, (2 * g + 1) * LANES:(2 * g + 2) * LANES].astype(BF16)
        nidx = lax.broadcasted_iota(I32, (8, nc), 1)
        tpos_c = past_len + _row_token((8, nc))
        cvalid = jnp.where(nidx < n_cmp, nidx * CMP_STRIDE + (CMP_BLOCK - 1), 2 ** 30) <= tpos_c
        n_lt = imp_ref.shape[1] // LANES
        jblk = lax.broadcasted_iota(I32, imp_ref.shape, 1)
        tpos_i = past_len + _row_token(imp_ref.shape)
        cur = tpos_i >> 6
        forced = (jblk == 0) | (jblk == cur) | (jblk == cur - 1)
        admissible = jblk * SLC_BLOCK <= tpos_i
        for g in range(B_KV_HEADS):
            pg = jnp.zeros((8, nc), F32)
            for half in range(2):
                t = g * 2 + half
                s = lax.dot_general(q_ref[t], kc_ref[g], _NT, preferred_element_type=F32)
                s = jnp.where(cvalid, s, NEG_BIG)
                e = jnp.where(cvalid, jnp.exp(s - jnp.max(s, axis=1, keepdims=True)), 0.0)
                l = jnp.sum(e, axis=1, keepdims=True)
                p = e / jnp.where(l > 0.0, l, 1.0)
                oc_ref[t] = jnp.dot(p.astype(BF16), vc_ref[g], preferred_element_type=F32)
                pg = pg + p
            pg = pg + pltpu.roll(pg, 4, 0)
            imp = _dot_f32(pg, ov_ref[...])
            imp_ref[...] = jnp.where(forced, jnp.inf, jnp.where(admissible, imp, -jnp.inf))
            thr, jlim = _topk_threshold(imp_ref, n_lt, n_sel)
            selb = jnp.concatenate(
                [jnp.where(_selected(imp_ref[:, c * LANES:(c + 1) * LANES], thr, jlim, lane + c * LANES), 1.0, 0.0)
                 for c in range(n_lt)], axis=1).astype(BF16)
            blk_row = lax.broadcasted_iota(I32, (imp_ref.shape[1], LANES), 0)
            key_col = lax.broadcasted_iota(I32, (imp_ref.shape[1], LANES), 1)

            def expand_body(c, carry):
                c0 = pl.multiple_of(c * LANES, LANES)
                ex = jnp.where(((key_col + c0) >> 6) == blk_row, 1.0, 0.0).astype(BF16)
                mask_ref[g, :, pl.ds(c0, LANES)] = jnp.dot(selb, ex, preferred_element_type=F32)
                return carry

            lax.fori_loop(0, mask_ref.shape[2] // LANES, expand_body, 0)

        _flash_init(m_ref, l_ref, acc_ref)
        wb = win_ref.shape[0]
        tok_w = _row_token((8, wb))
        widx = lax.broadcasted_iota(I32, (8, wb), 1)
        wvalid = (widx - wb) > (tok_w - WINDOW)
        nvalid = lane <= tok
        for g in range(B_KV_HEADS):
            kk = win_ref[:, (2 * g) * LANES:(2 * g + 1) * LANES].astype(BF16)
            vv = win_ref[:, (2 * g + 1) * LANES:(2 * g + 2) * LANES].astype(BF16)
            for half in range(2):
                t = g * 2 + half
                _flash_step(q_ref[t], kk, vv, wvalid, m_ref, l_ref, acc_ref, t)
                _flash_step(q_ref[t], kwn_ref[g], vwn_ref[g], nvalid, m_ref, l_ref, acc_ref, t)
                ow_ref[t] = acc_ref[t] / l_ref[t]
        _flash_init(m_ref, l_ref, acc_ref)

    for k in range(pps):
        c0 = pl.multiple_of((j * pps + k) * PAGE_SIZE, PAGE_SIZE)
        for g in range(B_KV_HEADS):
            valid = mask_ref[g, :, pl.ds(c0, PAGE_SIZE)] > 0.5
            kk = pages[k][:, (2 * g) * LANES:(2 * g + 1) * LANES].astype(BF16)
            vv = pages[k][:, (2 * g + 1) * LANES:(2 * g + 2) * LANES].astype(BF16)
            for half in range(2):
                _flash_step(q_ref[g * 2 + half], kk, vv, valid, m_ref, l_ref, acc_ref, g * 2 + half)

    @pl.when(j == pl.num_programs(1) - 1)
    def _():
        for g in range(B_KV_HEADS):
            valid = jnp.where(lane <= tok, mask_ref[g, :, past_len:past_len + LANES], 0.0) > 0.5
            for half in range(2):
                t = g * 2 + half
                _flash_step(q_ref[t], ksn_ref[g], vsn_ref[g], valid, m_ref, l_ref, acc_ref, t)
                o_ref[t] = (gate_ref[t, 0] * oc_ref[t] + gate_ref[t, 1] * (acc_ref[t] / l_ref[t])
                            + gate_ref[t, 2] * ow_ref[t])


def _nsa_sample(page_table, cache_slc, state_win, q_s, gates_s, p0, p1, pn, ks_new, vs_new, kw_new, vw_new,
                pps=8):
    nb, n_pages = page_table.shape
    pps = min(pps, n_pages)
    past_len = n_pages * PAGE_SIZE
    total = past_len + 4
    ncz = -(-total // CMP_STRIDE)
    nc = ncz - 1
    n_cmp = ncz - CMP_BLOCK // CMP_STRIDE + 1
    assert nc == n_pages * CHUNKS_PER_PAGE and n_cmp == nc
    n_slc = -(-total // SLC_BLOCK)
    imp_w = -(-n_slc // LANES) * LANES
    ov = _overlap_matrix(nc, n_cmp, imp_w)
    wb = state_win.shape[1]
    per_b3 = lambda b, j, pt: (b, 0, 0)
    per_b4 = lambda b, j, pt: (b, 0, 0, 0)
    once = pl.Buffered(1)
    kern = functools.partial(_nsa_sample_kernel, pps=pps, past_len=past_len, n_cmp=n_cmp,
                             n_sel=min(SLC_TOPN, n_slc))
    new_spec = pl.BlockSpec((None, B_KV_HEADS, LANES, HEAD_DIM), per_b4)
    grid_spec = pltpu.PrefetchScalarGridSpec(
        num_scalar_prefetch=1,
        grid=(nb, n_pages // pps),
        in_specs=[pl.BlockSpec((None, N_TILES_B, 8, HEAD_DIM), per_b4),
                  pl.BlockSpec((None, N_TILES_B, 3, 8, HEAD_DIM), lambda b, j, pt: (b, 0, 0, 0, 0)),
                  pl.BlockSpec((nc, B_KVW), lambda b, j, pt: (b, 0)),
                  pl.BlockSpec((nc, B_KVW), lambda b, j, pt: (b, 0)),
                  pl.BlockSpec((None, 8, B_KVW), per_b3),
                  pl.BlockSpec((nc, imp_w), lambda b, j, pt: (0, 0), pipeline_mode=once),
                  new_spec, new_spec,
                  pl.BlockSpec((None, wb, B_KVW), per_b3),
                  new_spec, new_spec]
        + _page_specs((PAGE_SIZE, B_KVW), pps),
        out_specs=pl.BlockSpec((None, N_TILES_B, 8, HEAD_DIM), per_b4),
        scratch_shapes=[pltpu.VMEM((B_KV_HEADS, nc, HEAD_DIM), BF16),
                        pltpu.VMEM((B_KV_HEADS, nc, HEAD_DIM), BF16),
                        pltpu.VMEM((8, imp_w), F32),
                        pltpu.VMEM((B_KV_HEADS, 8, past_len + LANES), F32),
                        pltpu.VMEM((N_TILES_B, 8, HEAD_DIM), F32),
                        pltpu.VMEM((N_TILES_B, 8, HEAD_DIM), F32),
                        pltpu.VMEM((N_TILES_B, 8, LANES), F32),
                        pltpu.VMEM((N_TILES_B, 8, LANES), F32),
                        pltpu.VMEM((N_TILES_B, 8, HEAD_DIM), F32)],
    )
    return pl.pallas_call(
        kern,
        grid_spec=grid_spec,
        out_shape=jax.ShapeDtypeStruct((nb, N_TILES_B, 8, HEAD_DIM), F32),
        compiler_params=_cparams(2),
        name="nsa_sample",
    )(page_table, q_s, gates_s, p0, p1, pn, ov, ks_new, vs_new,
      state_win.reshape(nb, wb, B_KVW), kw_new, vw_new,
      *([cache_slc.reshape(cache_slc.shape[0], PAGE_SIZE, B_KVW)] * pps))


def _nsa_sample_layer(x, cache_cmp, cache_slc, state_win, page_table, norm_g, w_in, q_g, k_g,
                      w_cmp, pe_cmp, w_o):
    nb, nt, _ = x.shape
    assert nt == 4
    n = nb * nt
    past_len = page_table.shape[1] * PAGE_SIZE
    pos = jnp.tile(past_len + jnp.arange(nt), nb)
    xf = x.reshape(n, D_MODEL)
    q, kvc, kvs, kvw, ks, vs, kw, vw, gates, z = _proj_b(
        xf, norm_g, _layout_w_b(w_in), q_g, k_g, _rope_tables(pos, ROT_DIM), n)
    wc, pe = _layout_cmp(w_cmp, pe_cmp)
    x_new = jnp.pad(kvc.reshape(nb, 1, nt * B_KVW), ((0, 0), (0, 7), (0, CHUNK_COLS - nt * B_KVW)))
    p0, p1, pn = _cmp_paged(page_table, cache_cmp, x_new, wc, pe)
    q_s = q.reshape(nb, nt, N_TILES_B, 2, HEAD_DIM).transpose(0, 2, 3, 1, 4).reshape(nb, N_TILES_B, 8, HEAD_DIM)
    gt = gates[:, :3 * N_HEADS].reshape(nb, nt, N_TILES_B, 2, 3).transpose(0, 2, 4, 3, 1)
    gates_s = jnp.broadcast_to(gt.reshape(nb, N_TILES_B, 3, 8, 1), (nb, N_TILES_B, 3, 8, HEAD_DIM))
    new_keys = lambda a: _pad_keys(a.reshape(B_KV_HEADS, nb, nt, HEAD_DIM).transpose(1, 0, 2, 3), 2)
    o_s = _nsa_sample(page_table, cache_slc, state_win, q_s, gates_s, p0, p1, pn,
                      new_keys(ks), new_keys(vs), new_keys(kw), new_keys(vw))
    o = o_s.reshape(nb, N_TILES_B, 2, nt, HEAD_DIM).transpose(0, 3, 1, 2, 4).reshape(n, WIDTH)
    y = _out_proj(xf, o, z, w_o, n)
    shp = (nb, nt, B_KV_HEADS, 2, HEAD_DIM)
    win_out = jnp.concatenate([state_win, kvw.reshape(shp)], axis=1)[:, nt:]
    return y.reshape(nb, nt, D_MODEL), kvc.reshape(shp), kvs.reshape(shp), win_out


def kernel(x_prompt, x_sample, cache_a_kv, cache_a_idx, cache_b_cmp_kv, cache_b_slc_kv, state_b_win_kv,
           page_table, a_norm, a_w_in, a_q_norm, a_k_norm, a_w_o, b_norm, b_w_in, b_q_norm, b_k_norm,
           b_cmp_w, b_cmp_pe, b_w_o):
    yp, a_kv_p, a_idx_p = _dsa_prompt_layer(x_prompt, a_norm, a_w_in, a_q_norm, a_k_norm, a_w_o)
    yp, b_cmp_p, b_slc_p, b_win_p = _nsa_prompt_layer(yp, b_norm, b_w_in, b_q_norm, b_k_norm,
                                                      b_cmp_w, b_cmp_pe, b_w_o)
    ys, a_kv_s, a_idx_s = _dsa_sample_layer(x_sample, cache_a_kv, cache_a_idx, page_table,
                                            a_norm, a_w_in, a_q_norm, a_k_norm, a_w_o)
    ys, b_cmp_s, b_slc_s, b_win_s = _nsa_sample_layer(ys, cache_b_cmp_kv, cache_b_slc_kv, state_b_win_kv,
                                                      page_table, b_norm, b_w_in, b_q_norm, b_k_norm,
                                                      b_cmp_w, b_cmp_pe, b_w_o)
    return (yp, ys, a_kv_p, a_idx_p, a_kv_s, a_idx_s, b_cmp_p, b_slc_p, b_win_p, b_cmp_s, b_slc_s, b_win_s)
```

```python
import functools

import numpy as np
import jax
import jax.numpy as jnp
from jax import lax
from jax.experimental import pallas as pl
from jax.experimental.pallas import tpu as pltpu

F32 = jnp.float32
BF16 = jnp.bfloat16
I32 = jnp.int32

D_MODEL = 1024
N_HEADS = 8
HEAD_DIM = 128
WIDTH = N_HEADS * HEAD_DIM
ROT_DIM = HEAD_DIM // 4
ROPE_THETA = 500000.0
EPS = 1e-6
ATTN_SCALE = HEAD_DIM ** -0.5
Q_SCALE = ATTN_SCALE * 1.4426950408889634
PAGE_SIZE = 128
A_KV_HEADS = 4
IDX_HEADS = 8
IDX_DIM = 64
IDX_ROT = IDX_DIM // 4
TOPK_MAX = 256
B_KV_HEADS = 2
GB = N_HEADS // B_KV_HEADS
CMP_BLOCK = 32
CMP_STRIDE = 16
SLC_BLOCK = 64
SLC_TOPN = 16
WINDOW = 512

LANES = 128
NEG_BIG = -1e30
KEY_NEG_INF = -2139095041
VMEM_LIMIT = 56 * 1024 * 1024

_NT = (((1,), (1,)), ((), ()))


def _cparams(n_grid_dims):
    return pltpu.CompilerParams(
        dimension_semantics=("arbitrary",) * n_grid_dims,
        vmem_limit_bytes=VMEM_LIMIT)


def _rope_tables(pos, rot_dim):
    half = rot_dim // 2
    freq = ROPE_THETA ** (-jnp.arange(half, dtype=F32) / half)
    ang = pos.astype(F32)[:, None] * freq
    cos, sin = jnp.cos(ang), jnp.sin(ang)
    t = pos.shape[0]
    one = jnp.ones((t, LANES - rot_dim), F32)
    zero = jnp.zeros((t, LANES - rot_dim), F32)
    zh = jnp.zeros((t, half), F32)
    c = jnp.concatenate([cos, cos, one], axis=1)
    s1 = jnp.concatenate([-sin, zh, zero], axis=1)
    s2 = jnp.concatenate([zh, sin, zero], axis=1)
    return jnp.stack([c, s1, s2])


def _rope(t, tab_ref, half):
    return (t * tab_ref[0] + pltpu.roll(t, LANES - half, 1) * tab_ref[1]
            + pltpu.roll(t, half, 1) * tab_ref[2])


def _head_norm(t, g):
    return t * lax.rsqrt(jnp.mean(t * t, axis=-1, keepdims=True) + EPS) * g


A_COLS = 4 * WIDTH + LANES


def _layout_w_a(w_in):
    o = 0
    wq = w_in[:, o:o + WIDTH]; o += WIDTH
    wkv = w_in[:, o:o + 2 * A_KV_HEADS * HEAD_DIM]; o += 2 * A_KV_HEADS * HEAD_DIM
    wiq = w_in[:, o:o + IDX_HEADS * IDX_DIM]; o += IDX_HEADS * IDX_DIM
    wik = w_in[:, o:o + IDX_DIM]; o += IDX_DIM
    wiw = w_in[:, o:o + IDX_HEADS]; o += IDX_HEADS
    wz = w_in[:, o:o + WIDTH]
    d = w_in.shape[0]
    wiq = jnp.pad(wiq.reshape(d, IDX_HEADS, IDX_DIM), ((0, 0), (0, 0), (0, LANES - IDX_DIM))).reshape(d, -1)
    wikw = jnp.concatenate([wik, wiw, jnp.zeros((d, LANES - IDX_DIM - IDX_HEADS), w_in.dtype)], axis=1)
    return jnp.concatenate([wq, wkv, wiq, wikw, wz], axis=1).astype(BF16)


def _proj_a_kernel(x_ref, g_ref, w_ref, qg_ref, kg_ref, rt_ref, it_ref,
                   q_out, kv_out, k_out, v_out, iq_out, ikw_out, ikp_out, z_out, xn_ref):
    x = x_ref[...]
    ms = jnp.mean(x * x, axis=-1, keepdims=True)
    xn_ref[...] = (x * lax.rsqrt(ms + EPS) * g_ref[...]).astype(BF16)

    def seg(c0):
        return jnp.dot(xn_ref[...], w_ref[:, c0:c0 + LANES], preferred_element_type=F32)

    qg = qg_ref[...]
    kg = kg_ref[...]
    for h in range(N_HEADS):
        t = _rope(_head_norm(seg(h * LANES), qg), rt_ref, ROT_DIM // 2)
        q_out[:, h * LANES:(h + 1) * LANES] = (t * Q_SCALE).astype(BF16)
    base = WIDTH
    for g in range(A_KV_HEADS):
        k = _rope(_head_norm(seg(base + (2 * g) * LANES), kg), rt_ref, ROT_DIM // 2)
        v = seg(base + (2 * g + 1) * LANES)
        kv_out[:, (2 * g) * LANES:(2 * g + 1) * LANES] = k
        kv_out[:, (2 * g + 1) * LANES:(2 * g + 2) * LANES] = v
        k_out[g] = k.astype(BF16)
        v_out[g] = v.astype(BF16)
    base = 2 * WIDTH
    for h in range(IDX_HEADS):
        t = _rope(seg(base + h * LANES), it_ref, IDX_ROT // 2)
        iq_out[:, h * LANES:(h + 1) * LANES] = t.astype(BF16)
    base = 3 * WIDTH
    t = _rope(seg(base), it_ref, IDX_ROT // 2)
    lane = lax.broadcasted_iota(I32, t.shape, 1)
    w_scale = (IDX_HEADS * IDX_DIM) ** -0.5
    ikw_out[...] = jnp.where(lane < IDX_DIM, t, t * w_scale)
    ikp_out[...] = jnp.where(lane < IDX_DIM, t, 0.0).astype(BF16)
    base = 3 * WIDTH + LANES
    for h in range(N_HEADS):
        z_out[:, h * LANES:(h + 1) * LANES] = seg(base + h * LANES)


def _proj_a(x, norm_g, w_a, q_g, k_g, rope_tab, idx_tab, tm):
    n = x.shape[0]
    period = rope_tab.shape[1] // tm
    row = lambda i: (i, 0)
    const = lambda i: (0, 0)
    tab = lambda i: (0, i % period, 0)
    hm = lambda i: (0, i, 0)
    return pl.pallas_call(
        _proj_a_kernel,
        grid=(n // tm,),
        in_specs=[
            pl.BlockSpec((tm, D_MODEL), row),
            pl.BlockSpec((1, D_MODEL), const),
            pl.BlockSpec((D_MODEL, A_COLS), const),
            pl.BlockSpec((1, HEAD_DIM), const),
            pl.BlockSpec((1, HEAD_DIM), const),
            pl.BlockSpec((3, tm, LANES), tab),
            pl.BlockSpec((3, tm, LANES), tab),
        ],
        out_specs=[
            pl.BlockSpec((tm, WIDTH), row),
            pl.BlockSpec((tm, 2 * A_KV_HEADS * HEAD_DIM), row),
            pl.BlockSpec((A_KV_HEADS, tm, HEAD_DIM), hm),
            pl.BlockSpec((A_KV_HEADS, tm, HEAD_DIM), hm),
            pl.BlockSpec((tm, IDX_HEADS * LANES), row),
            pl.BlockSpec((tm, LANES), row),
            pl.BlockSpec((tm, LANES), row),
            pl.BlockSpec((tm, WIDTH), row),
        ],
        out_shape=[
            jax.ShapeDtypeStruct((n, WIDTH), BF16),
            jax.ShapeDtypeStruct((n, 2 * A_KV_HEADS * HEAD_DIM), F32),
            jax.ShapeDtypeStruct((A_KV_HEADS, n, HEAD_DIM), BF16),
            jax.ShapeDtypeStruct((A_KV_HEADS, n, HEAD_DIM), BF16),
            jax.ShapeDtypeStruct((n, IDX_HEADS * LANES), BF16),
            jax.ShapeDtypeStruct((n, LANES), F32),
            jax.ShapeDtypeStruct((n, LANES), BF16),
            jax.ShapeDtypeStruct((n, WIDTH), F32),
        ],
        scratch_shapes=[pltpu.VMEM((tm, D_MODEL), BF16)],
        compiler_params=_cparams(1),
        name="proj_a",
    )(x, norm_g.reshape(1, -1), w_a, q_g.reshape(1, -1), k_g.reshape(1, -1), rope_tab, idx_tab)


def _out_proj_kernel(x_ref, o_ref, z_ref, w_ref, y_ref):
    z = z_ref[...]
    h = (o_ref[...] * (z * jax.nn.sigmoid(z))).astype(BF16)
    y_ref[...] = x_ref[...] + jnp.dot(h, w_ref[...], preferred_element_type=F32)


def _out_proj(x, o, z, w_o, tm):
    n = x.shape[0]
    row = lambda i: (i, 0)
    return pl.pallas_call(
        _out_proj_kernel,
        grid=(n // tm,),
        in_specs=[pl.BlockSpec((tm, D_MODEL), row), pl.BlockSpec((tm, WIDTH), row),
                  pl.BlockSpec((tm, WIDTH), row), pl.BlockSpec((WIDTH, D_MODEL), lambda i: (0, 0))],
        out_specs=pl.BlockSpec((tm, D_MODEL), row),
        out_shape=jax.ShapeDtypeStruct((n, D_MODEL), F32),
        compiler_params=_cparams(1),
        name="out_proj",
    )(x, o, z, w_o.astype(BF16))


def _float_key(x):
    bits = pltpu.bitcast(x, I32)
    return bits ^ ((bits >> 31) & 0x7FFFFFFF)


def _row_count(sc_ref, n_groups, group, row_chunk, pred, args):
    rows = sc_ref.shape[0]
    rc = min(row_chunk, rows)
    starts = list(range(0, rows, rc))
    parts = []
    for i in range(0, len(starts), 2):
        pair = starts[i:i + 2]
        a = [[x[r0:r0 + rc] if hasattr(x, "shape") and x.shape else x for x in args] for r0 in pair]

        def body(j, cs, pair=pair, a=a):
            cs = list(cs)
            for u in range(group):
                c0 = pl.multiple_of((j * group + u) * LANES, LANES)
                for n, r0 in enumerate(pair):
                    cs[n] = cs[n] + jnp.where(pred(sc_ref[r0:r0 + rc, pl.ds(c0, LANES)], c0, *a[n]), 1, 0)
            return tuple(cs)

        parts.extend(lax.fori_loop(0, n_groups, body, tuple(jnp.zeros((rc, LANES), I32) for _ in pair)))
    c = parts[0] if len(parts) == 1 else jnp.concatenate(parts, axis=0)
    return jnp.broadcast_to(jnp.sum(c, axis=1, keepdims=True), (rows, LANES))


def _topk_threshold(sc_ref, n_groups, group, k, row_chunk=64):
    rows = sc_ref.shape[0]
    count = functools.partial(_row_count, sc_ref, n_groups, group, row_chunk)

    def count_ge(t):
        return count(lambda blk, c0, thr: blk >= thr, [t])

    state = (jnp.int32(0), jnp.full((rows, LANES), -2 ** 31, I32), jnp.full((rows, LANES), 2 ** 30, I32))

    def unsettled(st):
        it, _, cnt = st
        return jnp.logical_and(it < 32, jnp.max(jnp.where(cnt != k, 1, 0)) > 0)

    def bit_pair(st):
        it, t, cnt = st
        for step in range(2):
            cand = t + (jnp.int32(1) << (31 - it - step))
            c = count_ge(cand)
            ok = c >= k
            t = jnp.where(ok, cand, t)
            cnt = jnp.where(ok, c, cnt)
        return it + 2, t, cnt

    _, t, n_ge = lax.while_loop(unsettled, bit_pair, state)
    thr = jnp.maximum(t, KEY_NEG_INF)
    short = thr == KEY_NEG_INF
    excess = jnp.where(short, 0, n_ge - k)
    big = jnp.full((rows, LANES), 2 ** 30, I32)

    def tie_limit():
        need = k - count(lambda blk, c0, thr: blk > thr, [thr])

        def tie_before(blk, c0, thr, cand):
            col = lax.broadcasted_iota(I32, blk.shape, 1) + c0
            return (blk == thr) & (col < cand)

        def jbody(it, jl):
            cand = jl + (jnp.int32(1) << (29 - it))
            return jnp.where(count(tie_before, [thr, cand]) <= need, cand, jl)
        return lax.fori_loop(0, 30, jbody, need * 0)

    jlim = lax.cond(jnp.max(excess) > 0, tie_limit, lambda: big)
    return thr, jnp.where(short, 0, jlim)


def _selected(key, thr, jlim, col):
    return (key > thr) | ((key == thr) & (col < jlim))


def _dsa_prompt_kernel(iq_ref, ikw_ref, ik_ref, q_ref, k_ref, v_ref, o_ref,
                       sc_ref, m_ref, l_ref, acc_ref, *, tq, tk, n_keep):
    i = pl.program_id(1)
    q0 = i * tq
    nkb = lax.div(q0 + tq + tk - 1, tk)
    tiles_per_kb = tk // LANES

    def score_body(j, carry):
        k0 = pl.multiple_of(j * tk, tk)
        ikb = ik_ref[pl.ds(k0, tk), :]
        acc = jnp.zeros((tq, tk), F32)
        for h in range(IDX_HEADS):
            lg = lax.dot_general(iq_ref[:, h * LANES:(h + 1) * LANES], ikb, _NT,
                                 preferred_element_type=F32)
            w = ikw_ref[:, IDX_DIM + h:IDX_DIM + h + 1]
            acc = acc + w * jnp.maximum(lg, 0.0)
        kpos = k0 + lax.broadcasted_iota(I32, (tq, tk), 1)
        tpos = q0 + lax.broadcasted_iota(I32, (tq, tk), 0)
        sc_ref[:, pl.ds(k0, tk)] = _float_key(jnp.where(kpos <= tpos, acc, -jnp.inf))
        return carry

    lax.fori_loop(0, nkb, score_body, 0)

    thr, jlim = _topk_threshold(sc_ref, nkb, tiles_per_kb, n_keep)

    half = tq // 2
    lane = lax.broadcasted_iota(I32, (half, LANES), 1)

    def bias_body(j, carry):
        c0 = pl.multiple_of(j * LANES, LANES)
        for r0 in (0, half):
            sel = _selected(sc_ref[r0:r0 + half, pl.ds(c0, LANES)], thr[r0:r0 + half], jlim[r0:r0 + half],
                            lane + c0)
            sc_ref[r0:r0 + half, pl.ds(c0, LANES)] = pltpu.bitcast(_mask_bias(sel), I32)
        return carry

    lax.fori_loop(0, nkb * tiles_per_kb, bias_body, 0)

    _flash_init(m_ref, l_ref, acc_ref)

    def attn_body(j, carry):
        k0 = pl.multiple_of(j * tk, tk)
        bias = [pltpu.bitcast(sc_ref[:, pl.ds(k0 + t * LANES, LANES)], F32) for t in range(tiles_per_kb)]
        for h in range(N_HEADS):
            g = h // (N_HEADS // A_KV_HEADS)
            _flash_step(q_ref[:, h * LANES:(h + 1) * LANES], k_ref[g, pl.ds(k0, tk), :],
                        v_ref[g, pl.ds(k0, tk), :], bias, m_ref, l_ref, acc_ref, h)
        return carry

    lax.fori_loop(0, nkb, attn_body, 0)
    for h in range(N_HEADS):
        o_ref[:, h * LANES:(h + 1) * LANES] = acc_ref[h] / l_ref[h]


def _dsa_prompt(iq, ikw, ikp, q, k_hm, v_hm, batch, seq, n_keep, tq=512, tk=512):
    tq = min(tq, seq)
    tk = min(tk, seq)
    nq = seq // tq
    n = batch * seq
    row = lambda b, i: (b * nq + i, 0)
    kern = functools.partial(_dsa_prompt_kernel, tq=tq, tk=tk, n_keep=n_keep)
    once = pl.Buffered(1)
    return pl.pallas_call(
        kern,
        grid=(batch, nq),
        in_specs=[
            pl.BlockSpec((tq, IDX_HEADS * LANES), row),
            pl.BlockSpec((tq, LANES), row),
            pl.BlockSpec((seq, LANES), lambda b, i: (b, 0), pipeline_mode=once),
            pl.BlockSpec((tq, WIDTH), row),
            pl.BlockSpec((A_KV_HEADS, seq, HEAD_DIM), lambda b, i: (0, b, 0), pipeline_mode=once),
            pl.BlockSpec((A_KV_HEADS, seq, HEAD_DIM), lambda b, i: (0, b, 0), pipeline_mode=once),
        ],
        out_specs=pl.BlockSpec((tq, WIDTH), row),
        out_shape=jax.ShapeDtypeStruct((n, WIDTH), F32),
        scratch_shapes=[
            pltpu.VMEM((tq, seq), I32),
            pltpu.VMEM((N_HEADS, tq, LANES), F32),
            pltpu.VMEM((N_HEADS, tq, LANES), F32),
            pltpu.VMEM((N_HEADS, tq, HEAD_DIM), F32),
        ],
        compiler_params=_cparams(2),
        name="dsa_prompt",
    )(iq, ikw, ikp, q, k_hm, v_hm)


def _dsa_prompt_layer(x, norm_g, w_in, q_g, k_g, w_o):
    b, s, _ = x.shape
    n = b * s
    pos = jnp.arange(s)
    tm = min(256, s)
    xf = x.reshape(n, D_MODEL)
    q, kv, k_hm, v_hm, iq, ikw, ikp, z = _proj_a(
        xf, norm_g, _layout_w_a(w_in), q_g, k_g, _rope_tables(pos, ROT_DIM), _rope_tables(pos, IDX_ROT), tm)
    o = _dsa_prompt(iq, ikw, ikp, q, k_hm, v_hm, b, s, min(TOPK_MAX, s // 4))
    y = _out_proj(xf, o, z, w_o, tm)
    return (y.reshape(b, s, D_MODEL), kv.reshape(b, s, A_KV_HEADS, 2, HEAD_DIM),
            ikw[:, :IDX_DIM].reshape(b, s, IDX_DIM))


B_KVW = 2 * B_KV_HEADS * HEAD_DIM
B_COLS = 2 * WIDTH + 3 * B_KVW + LANES


def _layout_w_b(w_in):
    o = WIDTH + 3 * B_KVW
    d = w_in.shape[0]
    wg = jnp.concatenate([w_in[:, o:o + 3 * N_HEADS], jnp.zeros((d, LANES - 3 * N_HEADS), w_in.dtype)], axis=1)
    return jnp.concatenate([w_in[:, :o], wg, w_in[:, o + 3 * N_HEADS:]], axis=1).astype(BF16)


def _proj_b_kernel(x_ref, g_ref, w_ref, qg_ref, kg_ref, rt_ref,
                   q_out, kvc_out, kvs_out, kvw_out, ks_out, vs_out, kw_out, vw_out, gate_out, z_out, xn_ref):
    x = x_ref[...]
    ms = jnp.mean(x * x, axis=-1, keepdims=True)
    xn_ref[...] = (x * lax.rsqrt(ms + EPS) * g_ref[...]).astype(BF16)

    def seg(c0):
        return jnp.dot(xn_ref[...], w_ref[:, c0:c0 + LANES], preferred_element_type=F32)

    qg = qg_ref[...]
    for h in range(N_HEADS):
        t = _rope(_head_norm(seg(h * LANES), qg), rt_ref, ROT_DIM // 2)
        q_out[:, h * LANES:(h + 1) * LANES] = (t * Q_SCALE).astype(BF16)
    branch_out = ((kvc_out, None, None), (kvs_out, ks_out, vs_out), (kvw_out, kw_out, vw_out))
    for br, (kv_out, k_out, v_out) in enumerate(branch_out):
        base = WIDTH + br * B_KVW
        kg = kg_ref[br:br + 1, :]
        for g in range(B_KV_HEADS):
            k = _rope(_head_norm(seg(base + (2 * g) * LANES), kg), rt_ref, ROT_DIM // 2)
            v = seg(base + (2 * g + 1) * LANES)
            kv_out[:, (2 * g) * LANES:(2 * g + 1) * LANES] = k
            kv_out[:, (2 * g + 1) * LANES:(2 * g + 2) * LANES] = v
            if k_out is not None:
                k_out[g] = k.astype(BF16)
                v_out[g] = v.astype(BF16)
    base = WIDTH + 3 * B_KVW
    gate_out[...] = jax.nn.sigmoid(seg(base))
    base += LANES
    for h in range(N_HEADS):
        z_out[:, h * LANES:(h + 1) * LANES] = seg(base + h * LANES)


def _proj_b(x, norm_g, w_b, q_g, k_g, rope_tab, tm):
    n = x.shape[0]
    period = rope_tab.shape[1] // tm
    row = lambda i: (i, 0)
    const = lambda i: (0, 0)
    hm = lambda i: (0, i, 0)
    kv_spec = pl.BlockSpec((tm, B_KVW), row)
    hm_spec = pl.BlockSpec((B_KV_HEADS, tm, HEAD_DIM), hm)
    kv_shape = jax.ShapeDtypeStruct((n, B_KVW), F32)
    hm_shape = jax.ShapeDtypeStruct((B_KV_HEADS, n, HEAD_DIM), BF16)
    return pl.pallas_call(
        _proj_b_kernel,
        grid=(n // tm,),
        in_specs=[
            pl.BlockSpec((tm, D_MODEL), row),
            pl.BlockSpec((1, D_MODEL), const),
            pl.BlockSpec((D_MODEL, B_COLS), const),
            pl.BlockSpec((1, HEAD_DIM), const),
            pl.BlockSpec((3, HEAD_DIM), const),
            pl.BlockSpec((3, tm, LANES), lambda i: (0, i % period, 0)),
        ],
        out_specs=[pl.BlockSpec((tm, WIDTH), row), kv_spec, kv_spec, kv_spec,
                   hm_spec, hm_spec, hm_spec, hm_spec,
                   pl.BlockSpec((tm, LANES), row), pl.BlockSpec((tm, WIDTH), row)],
        out_shape=[jax.ShapeDtypeStruct((n, WIDTH), BF16), kv_shape, kv_shape, kv_shape,
                   hm_shape, hm_shape, hm_shape, hm_shape,
                   jax.ShapeDtypeStruct((n, LANES), F32), jax.ShapeDtypeStruct((n, WIDTH), F32)],
        scratch_shapes=[pltpu.VMEM((tm, D_MODEL), BF16)],
        compiler_params=_cparams(1),
        name="proj_b",
    )(x, norm_g.reshape(1, -1), w_b, q_g.reshape(1, -1), k_g, rope_tab)


CHUNK_COLS = CMP_STRIDE * B_KVW


def _layout_cmp(w_cmp, pe_cmp):
    r = CMP_BLOCK // CMP_STRIDE
    eye_g = jnp.eye(B_KV_HEADS, dtype=w_cmp.dtype)
    eye_c = jnp.eye(2, dtype=w_cmp.dtype)
    wj = w_cmp.reshape(r, CMP_STRIDE, 2, HEAD_DIM, HEAD_DIM)
    w = jnp.einsum('jlcde,gh,ck->jlgcdhke', wj, eye_g, eye_c).reshape(r, CHUNK_COLS, B_KVW)
    pe = jnp.broadcast_to(pe_cmp.reshape(r, CMP_STRIDE, 1, 2, HEAD_DIM),
                          (r, CMP_STRIDE, B_KV_HEADS, 2, HEAD_DIM)).reshape(r, 1, CHUNK_COLS)
    return w.astype(BF16), pe


def _cmp_mm_kernel(x_ref, pe_ref, w_ref, p0_ref, p1_ref):
    x = x_ref[...]
    p0_ref[...] = jnp.dot((x + pe_ref[0]).astype(BF16), w_ref[0], preferred_element_type=F32)
    p1_ref[...] = jnp.dot((x + pe_ref[1]).astype(BF16), w_ref[1], preferred_element_type=F32)


def _cmp_mm(x, w, pe, tr):
    r = x.shape[0]
    row = lambda i: (i, 0)
    once = pl.Buffered(1)
    return pl.pallas_call(
        _cmp_mm_kernel,
        grid=(r // tr,),
        in_specs=[pl.BlockSpec((tr, CHUNK_COLS), row),
                  pl.BlockSpec((2, 1, CHUNK_COLS), lambda i: (0, 0, 0)),
                  pl.BlockSpec((2, CHUNK_COLS, B_KVW), lambda i: (0, 0, 0), pipeline_mode=once)],
        out_specs=[pl.BlockSpec((tr, B_KVW), row), pl.BlockSpec((tr, B_KVW), row)],
        out_shape=[jax.ShapeDtypeStruct((r, B_KVW), F32), jax.ShapeDtypeStruct((r, B_KVW), F32)],
        compiler_params=_cparams(1),
        name="cmp_mm",
    )(x, pe, w)


def _overlap_matrix(n_rows, n_cmp, n_cols):
    i = np.arange(n_rows)[:, None]
    j = np.arange(n_cols)[None, :]
    lo = np.maximum(i * CMP_STRIDE, j * SLC_BLOCK)
    hi = np.minimum(i * CMP_STRIDE + CMP_BLOCK, (j + 1) * SLC_BLOCK)
    ov = np.maximum(hi - lo, 0) / CMP_STRIDE
    ov = np.where(i < n_cmp, ov, 0.0)
    return jnp.asarray(ov, dtype=BF16)


def _dot_f32(a, b_bf16):
    hi = a.astype(BF16)
    r1 = a - hi.astype(F32)
    mid = r1.astype(BF16)
    lo = (r1 - mid.astype(F32)).astype(BF16)
    d = lambda p: jnp.dot(p, b_bf16, preferred_element_type=F32)
    return d(hi) + d(mid) + d(lo)


def _mask_bias(valid):
    return jnp.where(valid, 0.0, NEG_BIG)


def _flash_step(q, k, v, bias_tiles, m_ref, l_ref, acc_ref, h):
    s = lax.dot_general(q, k, _NT, preferred_element_type=F32)
    st = [s[:, t * LANES:(t + 1) * LANES] + b for t, b in enumerate(bias_tiles)]
    rows = s.shape[0]
    mt = functools.reduce(jnp.maximum, st)
    m_prev = m_ref[h]
    m_new = jnp.maximum(m_prev, jnp.broadcast_to(jnp.max(mt, axis=1, keepdims=True), (rows, LANES)))
    alpha = jnp.exp2(m_prev - m_new)
    pt = [jnp.exp2(x - m_new) for x in st]
    lt = functools.reduce(jnp.add, pt)
    l_ref[h] = alpha * l_ref[h] + jnp.broadcast_to(jnp.sum(lt, axis=1, keepdims=True), (rows, LANES))
    p = pt[0] if len(pt) == 1 else jnp.concatenate(pt, axis=1)
    acc_ref[h] = alpha * acc_ref[h] + jnp.dot(p.astype(BF16), v, preferred_element_type=F32)
    m_ref[h] = m_new


def _flash_init(m_ref, l_ref, acc_ref):
    m_ref[...] = jnp.full(m_ref.shape, NEG_BIG, F32)
    l_ref[...] = jnp.zeros(l_ref.shape, F32)
    acc_ref[...] = jnp.zeros(acc_ref.shape, F32)


def _topk_cols(xt_ref, sel_ref, k):
    nb, c = xt_ref.shape
    slabs = [(r0, _float_key(xt_ref[r0:r0 + 8, :])) for r0 in range(0, nb, 8)]
    row = lax.broadcasted_iota(I32, (8, c), 0)

    def count(pred):
        acc = jnp.zeros((8, c), I32)
        for r0, blk in slabs:
            acc = acc + jnp.where(pred(blk, r0), 1, 0)
        return jnp.broadcast_to(jnp.sum(acc, axis=0, keepdims=True), (8, c))

    def bit_body(it, t):
        cand = t + (jnp.int32(1) << (31 - it))
        return jnp.where(count(lambda blk, r0: blk >= cand) >= k, cand, t)

    zero = jnp.zeros((8, c), I32)
    thr = jnp.maximum(lax.fori_loop(0, 32, bit_body, jnp.full((8, c), -2 ** 31, I32)), KEY_NEG_INF)
    short = thr == KEY_NEG_INF
    n_gt = count(lambda blk, r0: blk > thr)
    excess = jnp.where(short, 0, count(lambda blk, r0: blk >= thr) - k)
    need = k - n_gt
    n_bits = max(1, int(nb).bit_length())

    def tie_limit():
        jl = zero
        for it in range(n_bits):
            cand = jl + (1 << (n_bits - 1 - it))
            cnt = count(lambda blk, r0: (blk == thr) & ((row + r0) < cand))
            jl = jnp.where(cnt <= need, cand, jl)
        return jl

    jlim = lax.cond(jnp.max(excess) > 0, tie_limit, lambda: jnp.full((8, c), 2 ** 30, I32))
    jlim = jnp.where(short, 0, jlim)
    for r0, blk in slabs:
        sel_ref[r0:r0 + 8, :] = jnp.where((blk > thr) | ((blk == thr) & ((row + r0) < jlim)), 1.0, 0.0)


def _nsa_prompt_kernel(q_ref, gate_ref, p0_ref, p1_ref, ov_ref, ks_ref, vs_ref, kw_ref, vw_ref, o_ref,
                       kc_ref, vc_ref, impt_ref, selt_ref, oc_ref, os_ref, m_ref, l_ref, acc_ref,
                       *, tq, tk, wk, n_cmp, n_sel):
    i = pl.program_id(1)
    q0 = i * tq
    nc = p0_ref.shape[0]

    @pl.when(i == 0)
    def _():
        kc = p0_ref[...] + pltpu.roll(p1_ref[...], nc - 1, 0)
        for g in range(B_KV_HEADS):
            kc_ref[g] = kc[:, (2 * g) * LANES:(2 * g + 1) * LANES].astype(BF16)
            vc_ref[g] = kc[:, (2 * g + 1) * LANES:(2 * g + 2) * LANES].astype(BF16)

    tpos_c = q0 + lax.broadcasted_iota(I32, (tq, nc), 0)
    nidx = lax.broadcasted_iota(I32, (tq, nc), 1)
    cvalid = jnp.where(nidx < n_cmp, nidx * CMP_STRIDE + (CMP_BLOCK - 1), 2 ** 30) <= tpos_c
    tpos = q0 + lax.broadcasted_iota(I32, (tq, LANES), 0)
    jblk = lax.broadcasted_iota(I32, (tq, LANES), 1)
    cur = tpos >> 6
    forced = (jblk == 0) | (jblk == cur) | (jblk == cur - 1)
    admissible = jblk * SLC_BLOCK <= tpos
    sel = []
    for g in range(B_KV_HEADS):
        pg = jnp.zeros((tq, nc), F32)
        for r in range(GB):
            h = g * GB + r
            s = lax.dot_general(q_ref[:, h * LANES:(h + 1) * LANES], kc_ref[g], _NT,
                                preferred_element_type=F32)
            s = jnp.where(cvalid, s, NEG_BIG)
            e = jnp.where(cvalid, jnp.exp2(s - jnp.max(s, axis=1, keepdims=True)), 0.0)
            l = jnp.sum(e, axis=1, keepdims=True)
            p = e / jnp.where(l > 0.0, l, 1.0)
            oc_ref[h] = jnp.dot(p.astype(BF16), vc_ref[g], preferred_element_type=F32)
            pg = pg + p
        imp = _dot_f32(pg, ov_ref[...])
        imp = jnp.where(forced, jnp.inf, jnp.where(admissible, imp, -jnp.inf))
        impt_ref[...] = imp.T
        _topk_cols(impt_ref, selt_ref, n_sel)
        sel.append(selt_ref[...].T.astype(BF16))

    nkb = lax.div(q0 + tq + tk - 1, tk)
    _flash_init(m_ref, l_ref, acc_ref)
    blk_row = lax.broadcasted_iota(I32, (LANES, tk), 0)
    key_col = lax.broadcasted_iota(I32, (LANES, tk), 1)
    tpos_k = q0 + lax.broadcasted_iota(I32, (tq, tk), 0)
    kcol = lax.broadcasted_iota(I32, (tq, tk), 1)

    def slc_body(j, carry):
        k0 = pl.multiple_of(j * tk, tk)
        expand = jnp.where(((key_col + k0) >> 6) == blk_row, 1.0, 0.0).astype(BF16)
        causal = (kcol + k0) <= tpos_k
        for g in range(B_KV_HEADS):
            hit = jnp.dot(sel[g], expand, preferred_element_type=F32)
            bias = _mask_bias(jnp.where(causal, hit, 0.0) > 0.5)
            bias = [bias[:, t * LANES:(t + 1) * LANES] for t in range(tk // LANES)]
            for r in range(GB):
                h = g * GB + r
                _flash_step(q_ref[:, h * LANES:(h + 1) * LANES], ks_ref[g, pl.ds(k0, tk), :],
                            vs_ref[g, pl.ds(k0, tk), :], bias, m_ref, l_ref, acc_ref, h)
        return carry

    lax.fori_loop(0, nkb, slc_body, 0)
    for h in range(N_HEADS):
        os_ref[h] = acc_ref[h] / l_ref[h]

    w0 = pl.multiple_of(jnp.maximum(q0 - WINDOW, 0), tq)
    kpos = w0 + lax.broadcasted_iota(I32, (tq, wk), 1)
    tpos_w = q0 + lax.broadcasted_iota(I32, (tq, wk), 0)
    bias_w = _mask_bias(jnp.where(kpos <= tpos_w, kpos, -2 ** 30) > tpos_w - WINDOW)
    for h in range(N_HEADS):
        g = h // GB
        s = lax.dot_general(q_ref[:, h * LANES:(h + 1) * LANES], kw_ref[g, pl.ds(w0, wk), :], _NT,
                            preferred_element_type=F32) + bias_w
        p = jnp.exp2(s - jnp.max(s, axis=1, keepdims=True))
        ow = (jnp.dot(p.astype(BF16), vw_ref[g, pl.ds(w0, wk), :], preferred_element_type=F32)
              / jnp.sum(p, axis=1, keepdims=True))
        o_ref[:, h * LANES:(h + 1) * LANES] = (gate_ref[:, 3 * h:3 * h + 1] * oc_ref[h]
                                               + gate_ref[:, 3 * h + 1:3 * h + 2] * os_ref[h]
                                               + gate_ref[:, 3 * h + 2:3 * h + 3] * ow)


def _nsa_prompt(q, gates, p0, p1, ks, vs, kw, vw, batch, seq, tq=256, tk=512):
    tq = min(tq, seq)
    tk = min(tk, seq)
    nq = seq // tq
    n = batch * seq
    nc = seq // CMP_STRIDE
    n_cmp = nc - CMP_BLOCK // CMP_STRIDE + 1
    n_slc = seq // SLC_BLOCK
    assert n_slc <= LANES and WINDOW % tq == 0
    ov = _overlap_matrix(nc, n_cmp, LANES)
    row = lambda b, i: (b * nq + i, 0)
    per_b = lambda b, i: (b, 0)
    hm_b = lambda b, i: (0, b, 0)
    once = pl.Buffered(1)
    kern = functools.partial(_nsa_prompt_kernel, tq=tq, tk=tk, wk=min(WINDOW + tq, seq), n_cmp=n_cmp,
                             n_sel=min(SLC_TOPN, n_slc))
    hm_spec = pl.BlockSpec((B_KV_HEADS, seq, HEAD_DIM), hm_b, pipeline_mode=once)
    return pl.pallas_call(
        kern,
        grid=(batch, nq),
        in_specs=[
            pl.BlockSpec((tq, WIDTH), row),
            pl.BlockSpec((tq, LANES), row),
            pl.BlockSpec((nc, B_KVW), per_b, pipeline_mode=once),
            pl.BlockSpec((nc, B_KVW), per_b, pipeline_mode=once),
            pl.BlockSpec((nc, LANES), lambda b, i: (0, 0), pipeline_mode=once),
            hm_spec, hm_spec, hm_spec, hm_spec,
        ],
        out_specs=pl.BlockSpec((tq, WIDTH), row),
        out_shape=jax.ShapeDtypeStruct((n, WIDTH), F32),
        scratch_shapes=[
            pltpu.VMEM((B_KV_HEADS, nc, HEAD_DIM), BF16),
            pltpu.VMEM((B_KV_HEADS, nc, HEAD_DIM), BF16),
            pltpu.VMEM((LANES, tq), F32),
            pltpu.VMEM((LANES, tq), F32),
            pltpu.VMEM((N_HEADS, tq, HEAD_DIM), F32),
            pltpu.VMEM((N_HEADS, tq, HEAD_DIM), F32),
            pltpu.VMEM((N_HEADS, tq, LANES), F32),
            pltpu.VMEM((N_HEADS, tq, LANES), F32),
            pltpu.VMEM((N_HEADS, tq, HEAD_DIM), F32),
        ],
        compiler_params=_cparams(2),
        name="nsa_prompt",
    )(q, gates, p0, p1, ov, ks, vs, kw, vw)


def _nsa_prompt_layer(x, norm_g, w_in, q_g, k_g, w_cmp, pe_cmp, w_o):
    b, s, _ = x.shape
    n = b * s
    tm = min(256, s)
    xf = x.reshape(n, D_MODEL)
    q, kvc, kvs, kvw, ks, vs, kw, vw, gates, z = _proj_b(
        xf, norm_g, _layout_w_b(w_in), q_g, k_g, _rope_tables(jnp.arange(s), ROT_DIM), tm)
    wc, pe = _layout_cmp(w_cmp, pe_cmp)
    nc = s // CMP_STRIDE
    p0, p1 = _cmp_mm(kvc.reshape(b * nc, CHUNK_COLS), wc, pe, min(128, nc))
    o = _nsa_prompt(q, gates, p0, p1, ks, vs, kw, vw, b, s)
    y = _out_proj(xf, o, z, w_o, tm)
    shp = (b, s, B_KV_HEADS, 2, HEAD_DIM)
    wlen = min(WINDOW, s)
    return (y.reshape(b, s, D_MODEL), kvc.reshape(shp), kvs.reshape(shp),
            kvw.reshape(shp)[:, s - wlen:])


def _page_specs(block, pages_per_step):
    return [pl.BlockSpec((None,) + block,
                         lambda b, j, pt, k=k: (pt[b, j * pages_per_step + k],) + (0,) * len(block))
            for k in range(pages_per_step)]


def _split_tiles(n_tiles, max_group=16):
    group = max(g for g in range(1, max_group + 1) if n_tiles % g == 0)
    return n_tiles // group, group


def _row_token(shape):
    return lax.broadcasted_iota(I32, shape, 0) & 3


def _dsa_sample_score_kernel(pt_ref, iq_ref, iw_ref, ikn_ref, *rest, pps, past_len, n_keep):
    pages = rest[:pps]
    sc_ref, thr_ref, jlim_ref = rest[pps:]
    j = pl.program_id(1)
    iq = iq_ref[:, :IDX_DIM]

    def score(keys_bf16):
        lg = lax.dot_general(iq, keys_bf16, _NT, preferred_element_type=F32)
        acc = jnp.zeros((8, LANES), F32)
        for h in range(IDX_HEADS):
            acc = acc + iw_ref[h * 8:(h + 1) * 8, :] * jnp.maximum(lg[h * 8:(h + 1) * 8, :], 0.0)
        return acc

    for k in range(pps):
        c0 = pl.multiple_of((j * pps + k) * PAGE_SIZE, PAGE_SIZE)
        sc_ref[:, pl.ds(c0, PAGE_SIZE)] = _float_key(score(pages[k][...].astype(BF16)))

    @pl.when(j == pl.num_programs(1) - 1)
    def _():
        s_new = score(ikn_ref[:, :IDX_DIM])
        col = lax.broadcasted_iota(I32, (8, LANES), 1)
        sc_ref[:, past_len:past_len + LANES] = _float_key(
            jnp.where(col <= _row_token((8, LANES)), s_new, -jnp.inf))
        n_groups, group = _split_tiles(sc_ref.shape[1] // LANES)
        thr, jlim = _topk_threshold(sc_ref, n_groups, group, n_keep)
        thr_ref[...] = thr
        jlim_ref[...] = jlim


def _dsa_sample_scores(page_table, cache_idx, iq_s, iw_s, ik_new, n_keep, pps=8):
    nb, n_pages = page_table.shape
    past_len = n_pages * PAGE_SIZE
    scw = past_len + LANES
    per_b = lambda b, j, pt: (b, 0, 0)
    kern = functools.partial(_dsa_sample_score_kernel, pps=pps, past_len=past_len, n_keep=n_keep)
    grid_spec = pltpu.PrefetchScalarGridSpec(
        num_scalar_prefetch=1,
        grid=(nb, n_pages // pps),
        in_specs=[pl.BlockSpec((None, 64, LANES), per_b),
                  pl.BlockSpec((None, 64, LANES), per_b),
                  pl.BlockSpec((None, LANES, LANES), per_b)] + _page_specs((PAGE_SIZE, IDX_DIM), pps),
        out_specs=[pl.BlockSpec((None, 8, scw), per_b),
                   pl.BlockSpec((None, 8, LANES), per_b),
                   pl.BlockSpec((None, 8, LANES), per_b)],
    )
    return pl.pallas_call(
        kern,
        grid_spec=grid_spec,
        out_shape=[jax.ShapeDtypeStruct((nb, 8, scw), I32),
                   jax.ShapeDtypeStruct((nb, 8, LANES), I32),
                   jax.ShapeDtypeStruct((nb, 8, LANES), I32)],
        compiler_params=_cparams(2),
        name="dsa_sample_scores",
    )(page_table, iq_s, iw_s, ik_new, *([cache_idx] * pps))


def _dsa_sample_attn_kernel(pt_ref, q_ref, sc_ref, thr_ref, jlim_ref, kn_ref, vn_ref, *rest, pps, past_len):
    pages = rest[:pps]
    o_ref, kbuf_ref, vbuf_ref, m_ref, l_ref, acc_ref = rest[pps:]
    j = pl.program_id(1)
    thr = thr_ref[...]
    jlim = jlim_ref[...]
    lane = lax.broadcasted_iota(I32, (8, LANES), 1)
    rows_per_token = 2 * A_KV_HEADS

    @pl.when(j == 0)
    def _():
        _flash_init(m_ref, l_ref, acc_ref)

    def bias_at(c0):
        return _mask_bias(_selected(sc_ref[:, pl.ds(c0, LANES)], thr, jlim, lane + c0))

    for k in range(pps):
        for g in range(A_KV_HEADS):
            rows = pl.ds(k * PAGE_SIZE, PAGE_SIZE)
            kbuf_ref[g, rows, :] = pages[k][pl.ds(2 * g, PAGE_SIZE, stride=rows_per_token), :].astype(BF16)
            vbuf_ref[g, rows, :] = pages[k][pl.ds(2 * g + 1, PAGE_SIZE, stride=rows_per_token), :].astype(BF16)
    bias = [bias_at(pl.multiple_of((j * pps + k) * PAGE_SIZE, PAGE_SIZE)) for k in range(pps)]
    for g in range(A_KV_HEADS):
        _flash_step(q_ref[g], kbuf_ref[g], vbuf_ref[g], bias, m_ref, l_ref, acc_ref, g)

    @pl.when(j == pl.num_programs(1) - 1)
    def _():
        bias_new = [bias_at(past_len)]
        for g in range(A_KV_HEADS):
            _flash_step(q_ref[g], kn_ref[g], vn_ref[g], bias_new, m_ref, l_ref, acc_ref, g)
            o_ref[g] = acc_ref[g] / l_ref[g]


def _dsa_sample_attn(page_table, cache_kv, q_s, scores, thr, jlim, k_new, v_new, pps=8):
    nb, n_pages = page_table.shape
    pps = min(pps, n_pages)
    past_len = n_pages * PAGE_SIZE
    scw = scores.shape[-1]
    per_b3 = lambda b, j, pt: (b, 0, 0)
    per_b4 = lambda b, j, pt: (b, 0, 0, 0)
    page_rows = PAGE_SIZE * 2 * A_KV_HEADS
    kern = functools.partial(_dsa_sample_attn_kernel, pps=pps, past_len=past_len)
    grid_spec = pltpu.PrefetchScalarGridSpec(
        num_scalar_prefetch=1,
        grid=(nb, n_pages // pps),
        in_specs=[pl.BlockSpec((None, A_KV_HEADS, 8, HEAD_DIM), per_b4),
                  pl.BlockSpec((None, 8, scw), per_b3),
                  pl.BlockSpec((None, 8, LANES), per_b3),
                  pl.BlockSpec((None, 8, LANES), per_b3),
                  pl.BlockSpec((None, A_KV_HEADS, LANES, HEAD_DIM), per_b4),
                  pl.BlockSpec((None, A_KV_HEADS, LANES, HEAD_DIM), per_b4)]
        + _page_specs((page_rows, HEAD_DIM), pps),
        out_specs=pl.BlockSpec((None, A_KV_HEADS, 8, HEAD_DIM), per_b4),
        scratch_shapes=[pltpu.VMEM((A_KV_HEADS, pps * PAGE_SIZE, HEAD_DIM), BF16),
                        pltpu.VMEM((A_KV_HEADS, pps * PAGE_SIZE, HEAD_DIM), BF16),
                        pltpu.VMEM((A_KV_HEADS, 8, LANES), F32),
                        pltpu.VMEM((A_KV_HEADS, 8, LANES), F32),
                        pltpu.VMEM((A_KV_HEADS, 8, HEAD_DIM), F32)],
    )
    return pl.pallas_call(
        kern,
        grid_spec=grid_spec,
        out_shape=jax.ShapeDtypeStruct((nb, A_KV_HEADS, 8, HEAD_DIM), F32),
        compiler_params=_cparams(2),
        name="dsa_sample_attn",
    )(page_table, q_s, scores, thr, jlim, k_new, v_new,
      *([cache_kv.reshape(cache_kv.shape[0], page_rows, HEAD_DIM)] * pps))


def _pad_keys(a, axis):
    pad = [(0, 0)] * a.ndim
    pad[axis] = (0, LANES - a.shape[axis])
    return jnp.pad(a, pad)


def _dsa_sample_layer(x, cache_kv, cache_idx, page_table, norm_g, w_in, q_g, k_g, w_o):
    nb, nt, _ = x.shape
    assert nt == 4
    n = nb * nt
    past_len = page_table.shape[1] * PAGE_SIZE
    pos = jnp.tile(past_len + jnp.arange(nt), nb)
    xf = x.reshape(n, D_MODEL)
    q, kv, k_hm, v_hm, iq, ikw, ikp, z = _proj_a(
        xf, norm_g, _layout_w_a(w_in), q_g, k_g, _rope_tables(pos, ROT_DIM), _rope_tables(pos, IDX_ROT), n)
    ga = N_HEADS // A_KV_HEADS
    iq_s = iq.reshape(nb, nt, IDX_HEADS, LANES).transpose(0, 2, 1, 3)
    iq_s = jnp.concatenate([iq_s, iq_s], axis=2).reshape(nb, 64, LANES)
    iw = ikw[:, IDX_DIM:IDX_DIM + IDX_HEADS].reshape(nb, nt, IDX_HEADS).transpose(0, 2, 1)
    iw_s = jnp.broadcast_to(jnp.concatenate([iw, iw], axis=2).reshape(nb, 64, 1), (nb, 64, LANES))
    ik_new = _pad_keys(ikp.reshape(nb, nt, LANES), 1)
    q_s = q.reshape(nb, nt, A_KV_HEADS, ga, HEAD_DIM).transpose(0, 2, 3, 1, 4).reshape(nb, A_KV_HEADS, 8, HEAD_DIM)
    k_new = _pad_keys(k_hm.reshape(A_KV_HEADS, nb, nt, HEAD_DIM).transpose(1, 0, 2, 3), 2)
    v_new = _pad_keys(v_hm.reshape(A_KV_HEADS, nb, nt, HEAD_DIM).transpose(1, 0, 2, 3), 2)
    n_keep = min(TOPK_MAX, (past_len + nt) // 4)
    scores, thr, jlim = _dsa_sample_scores(page_table, cache_idx, iq_s, iw_s, ik_new, n_keep)
    o_s = _dsa_sample_attn(page_table, cache_kv, q_s, scores, thr, jlim, k_new, v_new)
    o = o_s.reshape(nb, A_KV_HEADS, ga, nt, HEAD_DIM).transpose(0, 3, 1, 2, 4).reshape(n, WIDTH)
    y = _out_proj(xf, o, z, w_o, n)
    return (y.reshape(nb, nt, D_MODEL), kv.reshape(nb, nt, A_KV_HEADS, 2, HEAD_DIM),
            ikw[:, :IDX_DIM].reshape(nb, nt, IDX_DIM))


CHUNKS_PER_PAGE = PAGE_SIZE // CMP_STRIDE


def _cmp_paged_kernel(pt_ref, xn_ref, pe_ref, w_ref, *rest, pps, k_chunk):
    pages = rest[:pps]
    p0_ref, p1_ref, pn_ref, xs_ref = rest[pps:]
    for k in range(pps):
        xs_ref[k * CHUNKS_PER_PAGE:(k + 1) * CHUNKS_PER_PAGE, :] = pages[k][...]
    rows = xs_ref.shape[0]
    acc0 = jnp.zeros((rows, B_KVW), F32)
    acc1 = jnp.zeros((rows, B_KVW), F32)
    accn = jnp.zeros((8, B_KVW), F32)
    for c in range(0, CHUNK_COLS, k_chunk):
        x = xs_ref[:, c:c + k_chunk]
        acc0 = acc0 + jnp.dot((x + pe_ref[0, :, c:c + k_chunk]).astype(BF16), w_ref[0, c:c + k_chunk, :],
                              preferred_element_type=F32)
        acc1 = acc1 + jnp.dot((x + pe_ref[1, :, c:c + k_chunk]).astype(BF16), w_ref[1, c:c + k_chunk, :],
                              preferred_element_type=F32)
        accn = accn + jnp.dot((xn_ref[:, c:c + k_chunk] + pe_ref[1, :, c:c + k_chunk]).astype(BF16),
                              w_ref[1, c:c + k_chunk, :], preferred_element_type=F32)
    p0_ref[...] = acc0
    p1_ref[...] = acc1
    pn_ref[...] = accn


def _cmp_paged(page_table, cache_cmp, x_new, w, pe, pps=32, k_chunk=1024):
    nb, n_pages = page_table.shape
    pps = min(pps, n_pages)
    steps = n_pages // pps
    rows = pps * CHUNKS_PER_PAGE
    kern = functools.partial(_cmp_paged_kernel, pps=pps, k_chunk=k_chunk)
    once = pl.Buffered(1)
    grid_spec = pltpu.PrefetchScalarGridSpec(
        num_scalar_prefetch=1,
        grid=(nb, steps),
        in_specs=[pl.BlockSpec((None, 8, CHUNK_COLS), lambda b, j, pt: (b, 0, 0)),
                  pl.BlockSpec((2, 1, CHUNK_COLS), lambda b, j, pt: (0, 0, 0)),
                  pl.BlockSpec((2, CHUNK_COLS, B_KVW), lambda b, j, pt: (0, 0, 0), pipeline_mode=once)]
        + _page_specs((CHUNKS_PER_PAGE, CHUNK_COLS), pps),
        out_specs=[pl.BlockSpec((rows, B_KVW), lambda b, j, pt: (b * steps + j, 0)),
                   pl.BlockSpec((rows, B_KVW), lambda b, j, pt: (b * steps + j, 0)),
                   pl.BlockSpec((None, 8, B_KVW), lambda b, j, pt: (b, 0, 0))],
        scratch_shapes=[pltpu.VMEM((rows, CHUNK_COLS), F32)],
    )
    n_chunks = n_pages * CHUNKS_PER_PAGE
    return pl.pallas_call(
        kern,
        grid_spec=grid_spec,
        out_shape=[jax.ShapeDtypeStruct((nb * n_chunks, B_KVW), F32),
                   jax.ShapeDtypeStruct((nb * n_chunks, B_KVW), F32),
                   jax.ShapeDtypeStruct((nb, 8, B_KVW), F32)],
        compiler_params=_cparams(2),
        name="cmp_paged",
    )(page_table, x_new, pe, w,
      *([cache_cmp.reshape(cache_cmp.shape[0], CHUNKS_PER_PAGE, CHUNK_COLS)] * pps))


QROWS_B = GB * 4


def _nsa_sample_kernel(pt_ref, q_ref, gate_ref, p0_ref, p1_ref, pn_ref, ov_ref, ksn_ref, vsn_ref,
                       win_ref, kwn_ref, vwn_ref, *rest, pps, past_len, n_cmp, n_sel):
    pages = rest[:pps]
    (o_ref, kc_ref, vc_ref, imp_ref, bias_ref, oc_ref, ow_ref, kbuf_ref, vbuf_ref,
     m_ref, l_ref, acc_ref) = rest[pps:]
    j = pl.program_id(1)
    nc = p0_ref.shape[0]
    tok = _row_token((QROWS_B, LANES))
    lane = lax.broadcasted_iota(I32, (QROWS_B, LANES), 1)
    rows_per_token = 2 * B_KV_HEADS

    @pl.when(j == 0)
    def _():
        rown = lax.broadcasted_iota(I32, (nc, B_KVW), 0)
        kc = p0_ref[...] + jnp.where(rown == nc - 1, jnp.broadcast_to(pn_ref[0:1, :], (nc, B_KVW)),
                                     pltpu.roll(p1_ref[...], nc - 1, 0))
        for g in range(B_KV_HEADS):
            kc_ref[g] = kc[:, (2 * g) * LANES:(2 * g + 1) * LANES].astype(BF16)
            vc_ref[g] = kc[:, (2 * g + 1) * LANES:(2 * g + 2) * LANES].astype(BF16)
        nidx = lax.broadcasted_iota(I32, (QROWS_B, nc), 1)
        tpos_c = past_len + _row_token((QROWS_B, nc))
        cvalid = jnp.where(nidx < n_cmp, nidx * CMP_STRIDE + (CMP_BLOCK - 1), 2 ** 30) <= tpos_c
        n_lt = imp_ref.shape[1] // LANES
        jblk = lax.broadcasted_iota(I32, imp_ref.shape, 1)
        tpos_i = past_len + _row_token(imp_ref.shape)
        cur = tpos_i >> 6
        forced = (jblk == 0) | (jblk == cur) | (jblk == cur - 1)
        admissible = jblk * SLC_BLOCK <= tpos_i
        blk_row = lax.broadcasted_iota(I32, (imp_ref.shape[1], LANES), 0)
        key_col = lax.broadcasted_iota(I32, (imp_ref.shape[1], LANES), 1)
        wb = win_ref.shape[0] // rows_per_token
        widx = lax.broadcasted_iota(I32, (QROWS_B, wb), 1)
        bias_w = jnp.concatenate([_mask_bias((widx - wb) > (_row_token((QROWS_B, wb)) - WINDOW)),
                                  _mask_bias(lane <= tok)], axis=1)
        for g in range(B_KV_HEADS):
            q = q_ref[g]
            s = lax.dot_general(q, kc_ref[g], _NT, preferred_element_type=F32)
            s = jnp.where(cvalid, s, NEG_BIG)
            e = jnp.where(cvalid, jnp.exp2(s - jnp.max(s, axis=1, keepdims=True)), 0.0)
            l = jnp.sum(e, axis=1, keepdims=True)
            p = e / jnp.where(l > 0.0, l, 1.0)
            oc_ref[g] = jnp.dot(p.astype(BF16), vc_ref[g], preferred_element_type=F32)
            pg = p[0:8] + p[8:16]
            pg = pg + pltpu.roll(pg, 4, 0)
            imp = _dot_f32(jnp.concatenate([pg, pg], axis=0), ov_ref[...])
            imp_ref[...] = _float_key(jnp.where(forced, jnp.inf, jnp.where(admissible, imp, -jnp.inf)))
            thr, jlim = _topk_threshold(imp_ref, 1, n_lt, n_sel)
            selb = jnp.concatenate(
                [jnp.where(_selected(imp_ref[:, c * LANES:(c + 1) * LANES], thr, jlim, lane + c * LANES), 1.0, 0.0)
                 for c in range(n_lt)], axis=1).astype(BF16)

            def expand_body(c, carry, g=g, selb=selb):
                c0 = pl.multiple_of(c * LANES, LANES)
                ex = jnp.where(((key_col + c0) >> 6) == blk_row, 1.0, 0.0).astype(BF16)
                hit = jnp.dot(selb, ex, preferred_element_type=F32)
                visible = jnp.where((lane + c0) <= (past_len + tok), hit, 0.0) > 0.5
                bias_ref[g, :, pl.ds(c0, LANES)] = _mask_bias(visible)
                return carry

            lax.fori_loop(0, bias_ref.shape[2] // LANES, expand_body, 0)

            kw = jnp.concatenate([win_ref[pl.ds(2 * g, wb, stride=rows_per_token), :].astype(BF16), kwn_ref[g]],
                                 axis=0)
            vw = jnp.concatenate([win_ref[pl.ds(2 * g + 1, wb, stride=rows_per_token), :].astype(BF16),
                                  vwn_ref[g]], axis=0)
            sw = lax.dot_general(q, kw, _NT, preferred_element_type=F32) + bias_w
            pw = jnp.exp2(sw - jnp.max(sw, axis=1, keepdims=True))
            ow_ref[g] = (jnp.dot(pw.astype(BF16), vw, preferred_element_type=F32)
                         / jnp.sum(pw, axis=1, keepdims=True))
        _flash_init(m_ref, l_ref, acc_ref)

    for k in range(pps):
        for g in range(B_KV_HEADS):
            rows = pl.ds(k * PAGE_SIZE, PAGE_SIZE)
            kbuf_ref[g, rows, :] = pages[k][pl.ds(2 * g, PAGE_SIZE, stride=rows_per_token), :].astype(BF16)
            vbuf_ref[g, rows, :] = pages[k][pl.ds(2 * g + 1, PAGE_SIZE, stride=rows_per_token), :].astype(BF16)
    for g in range(B_KV_HEADS):
        bias = [bias_ref[g, :, pl.ds(pl.multiple_of((j * pps + k) * PAGE_SIZE, PAGE_SIZE), PAGE_SIZE)]
                for k in range(pps)]
        _flash_step(q_ref[g], kbuf_ref[g], vbuf_ref[g], bias, m_ref, l_ref, acc_ref, g)

    @pl.when(j == pl.num_programs(1) - 1)
    def _():
        for g in range(B_KV_HEADS):
            _flash_step(q_ref[g], ksn_ref[g], vsn_ref[g], [bias_ref[g, :, past_len:past_len + LANES]],
                        m_ref, l_ref, acc_ref, g)
            o_ref[g] = (gate_ref[g, 0] * oc_ref[g] + gate_ref[g, 1] * (acc_ref[g] / l_ref[g])
                        + gate_ref[g, 2] * ow_ref[g])


def _nsa_sample(page_table, cache_slc, state_win, q_s, gates_s, p0, p1, pn, ks_new, vs_new, kw_new, vw_new,
                pps=16):
    nb, n_pages = page_table.shape
    pps = min(pps, n_pages)
    past_len = n_pages * PAGE_SIZE
    total = past_len + 4
    ncz = -(-total // CMP_STRIDE)
    nc = ncz - 1
    n_cmp = ncz - CMP_BLOCK // CMP_STRIDE + 1
    assert nc == n_pages * CHUNKS_PER_PAGE and n_cmp == nc
    n_slc = -(-total // SLC_BLOCK)
    imp_w = -(-n_slc // LANES) * LANES
    ov = _overlap_matrix(nc, n_cmp, imp_w)
    wb = state_win.shape[1]
    slabs = 2 * B_KV_HEADS
    per_b3 = lambda b, j, pt: (b, 0, 0)
    per_b4 = lambda b, j, pt: (b, 0, 0, 0)
    once = pl.Buffered(1)
    kern = functools.partial(_nsa_sample_kernel, pps=pps, past_len=past_len, n_cmp=n_cmp,
                             n_sel=min(SLC_TOPN, n_slc))
    new_spec = pl.BlockSpec((None, B_KV_HEADS, LANES, HEAD_DIM), per_b4)
    grid_spec = pltpu.PrefetchScalarGridSpec(
        num_scalar_prefetch=1,
        grid=(nb, n_pages // pps),
        in_specs=[pl.BlockSpec((None, B_KV_HEADS, QROWS_B, HEAD_DIM), per_b4),
                  pl.BlockSpec((None, B_KV_HEADS, 3, QROWS_B, HEAD_DIM), lambda b, j, pt: (b, 0, 0, 0, 0)),
                  pl.BlockSpec((nc, B_KVW), lambda b, j, pt: (b, 0)),
                  pl.BlockSpec((nc, B_KVW), lambda b, j, pt: (b, 0)),
                  pl.BlockSpec((None, 8, B_KVW), per_b3),
                  pl.BlockSpec((nc, imp_w), lambda b, j, pt: (0, 0), pipeline_mode=once),
                  new_spec, new_spec,
                  pl.BlockSpec((None, wb * slabs, HEAD_DIM), per_b3),
                  new_spec, new_spec]
        + _page_specs((PAGE_SIZE * slabs, HEAD_DIM), pps),
        out_specs=pl.BlockSpec((None, B_KV_HEADS, QROWS_B, HEAD_DIM), per_b4),
        scratch_shapes=[pltpu.VMEM((B_KV_HEADS, nc, HEAD_DIM), BF16),
                        pltpu.VMEM((B_KV_HEADS, nc, HEAD_DIM), BF16),
                        pltpu.VMEM((QROWS_B, imp_w), I32),
                        pltpu.VMEM((B_KV_HEADS, QROWS_B, past_len + LANES), F32),
                        pltpu.VMEM((B_KV_HEADS, QROWS_B, HEAD_DIM), F32),
                        pltpu.VMEM((B_KV_HEADS, QROWS_B, HEAD_DIM), F32),
                        pltpu.VMEM((B_KV_HEADS, pps * PAGE_SIZE, HEAD_DIM), BF16),
                        pltpu.VMEM((B_KV_HEADS, pps * PAGE_SIZE, HEAD_DIM), BF16),
                        pltpu.VMEM((B_KV_HEADS, QROWS_B, LANES), F32),
                        pltpu.VMEM((B_KV_HEADS, QROWS_B, LANES), F32),
                        pltpu.VMEM((B_KV_HEADS, QROWS_B, HEAD_DIM), F32)],
    )
    return pl.pallas_call(
        kern,
        grid_spec=grid_spec,
        out_shape=jax.ShapeDtypeStruct((nb, B_KV_HEADS, QROWS_B, HEAD_DIM), F32),
        compiler_params=_cparams(2),
        name="nsa_sample",
    )(page_table, q_s, gates_s, p0, p1, pn, ov, ks_new, vs_new,
      state_win.reshape(nb, wb * slabs, HEAD_DIM), kw_new, vw_new,
      *([cache_slc.reshape(cache_slc.shape[0], PAGE_SIZE * slabs, HEAD_DIM)] * pps))


def _nsa_sample_layer(x, cache_cmp, cache_slc, state_win, page_table, norm_g, w_in, q_g, k_g,
                      w_cmp, pe_cmp, w_o):
    nb, nt, _ = x.shape
    assert nt == 4
    n = nb * nt
    past_len = page_table.shape[1] * PAGE_SIZE
    pos = jnp.tile(past_len + jnp.arange(nt), nb)
    xf = x.reshape(n, D_MODEL)
    q, kvc, kvs, kvw, ks, vs, kw, vw, gates, z = _proj_b(
        xf, norm_g, _layout_w_b(w_in), q_g, k_g, _rope_tables(pos, ROT_DIM), n)
    wc, pe = _layout_cmp(w_cmp, pe_cmp)
    x_new = jnp.pad(kvc.reshape(nb, 1, nt * B_KVW), ((0, 0), (0, 7), (0, CHUNK_COLS - nt * B_KVW)))
    p0, p1, pn = _cmp_paged(page_table, cache_cmp, x_new, wc, pe)
    q_s = q.reshape(nb, nt, B_KV_HEADS, GB, HEAD_DIM).transpose(0, 2, 3, 1, 4).reshape(nb, B_KV_HEADS, QROWS_B, HEAD_DIM)
    gt = gates[:, :3 * N_HEADS].reshape(nb, nt, B_KV_HEADS, GB, 3).transpose(0, 2, 4, 3, 1)
    gates_s = jnp.broadcast_to(gt.reshape(nb, B_KV_HEADS, 3, QROWS_B, 1), (nb, B_KV_HEADS, 3, QROWS_B, HEAD_DIM))
    new_keys = lambda a: _pad_keys(a.reshape(B_KV_HEADS, nb, nt, HEAD_DIM).transpose(1, 0, 2, 3), 2)
    o_s = _nsa_sample(page_table, cache_slc, state_win, q_s, gates_s, p0, p1, pn,
                      new_keys(ks), new_keys(vs), new_keys(kw), new_keys(vw))
    o = o_s.reshape(nb, B_KV_HEADS, GB, nt, HEAD_DIM).transpose(0, 3, 1, 2, 4).reshape(n, WIDTH)
    y = _out_proj(xf, o, z, w_o, n)
    shp = (nb, nt, B_KV_HEADS, 2, HEAD_DIM)
    win_out = jnp.concatenate([state_win, kvw.reshape(shp)], axis=1)[:, nt:]
    return y.reshape(nb, nt, D_MODEL), kvc.reshape(shp), kvs.reshape(shp), win_out


def kernel(x_prompt, x_sample, cache_a_kv, cache_a_idx, cache_b_cmp_kv, cache_b_slc_kv, state_b_win_kv,
           page_table, a_norm, a_w_in, a_q_norm, a_k_norm, a_w_o, b_norm, b_w_in, b_q_norm, b_k_norm,
           b_cmp_w, b_cmp_pe, b_w_o):
    yp, a_kv_p, a_idx_p = _dsa_prompt_layer(x_prompt, a_norm, a_w_in, a_q_norm, a_k_norm, a_w_o)
    yp, b_cmp_p, b_slc_p, b_win_p = _nsa_prompt_layer(yp, b_norm, b_w_in, b_q_norm, b_k_norm,
                                                      b_cmp_w, b_cmp_pe, b_w_o)
    ys, a_kv_s, a_idx_s = _dsa_sample_layer(x_sample, cache_a_kv, cache_a_idx, page_table,
                                            a_norm, a_w_in, a_q_norm, a_k_norm, a_w_o)
    ys, b_cmp_s, b_slc_s, b_win_s = _nsa_sample_layer(ys, cache_b_cmp_kv, cache_b_slc_kv, state_b_win_kv,
                                                      page_table, b_norm, b_w_in, b_q_norm, b_k_norm,
                                                      b_cmp_w, b_cmp_pe, b_w_o)
    return (yp, ys, a_kv_p, a_idx_p, a_kv_s, a_idx_s, b_cmp_p, b_slc_p, b_win_p, b_cmp_s, b_slc_s, b_win_s)
```

```python
import functools

import numpy as np
import jax
import jax.numpy as jnp
from jax import lax
from jax.experimental import pallas as pl
from jax.experimental.pallas import tpu as pltpu

F32 = jnp.float32
BF16 = jnp.bfloat16
I32 = jnp.int32

D_MODEL = 1024
N_HEADS = 8
HEAD_DIM = 128
WIDTH = N_HEADS * HEAD_DIM
ROT_DIM = HEAD_DIM // 4
ROPE_THETA = 500000.0
EPS = 1e-6
ATTN_SCALE = HEAD_DIM ** -0.5
Q_SCALE = ATTN_SCALE * 1.4426950408889634
PAGE_SIZE = 128
A_KV_HEADS = 4
IDX_HEADS = 8
IDX_DIM = 64
IDX_ROT = IDX_DIM // 4
TOPK_MAX = 256
B_KV_HEADS = 2
GB = N_HEADS // B_KV_HEADS
CMP_BLOCK = 32
CMP_STRIDE = 16
SLC_BLOCK = 64
SLC_TOPN = 16
WINDOW = 512

LANES = 128
NEG_BIG = -1e30
KEY_NEG_INF = -2139095041
VMEM_LIMIT = 56 * 1024 * 1024

_NT = (((1,), (1,)), ((), ()))


def _cparams(n_grid_dims):
    return pltpu.CompilerParams(
        dimension_semantics=("arbitrary",) * n_grid_dims,
        vmem_limit_bytes=VMEM_LIMIT)


def _rope_tables(pos, rot_dim):
    half = rot_dim // 2
    freq = ROPE_THETA ** (-jnp.arange(half, dtype=F32) / half)
    ang = pos.astype(F32)[:, None] * freq
    cos, sin = jnp.cos(ang), jnp.sin(ang)
    t = pos.shape[0]
    one = jnp.ones((t, LANES - rot_dim), F32)
    zero = jnp.zeros((t, LANES - rot_dim), F32)
    zh = jnp.zeros((t, half), F32)
    c = jnp.concatenate([cos, cos, one], axis=1)
    s1 = jnp.concatenate([-sin, zh, zero], axis=1)
    s2 = jnp.concatenate([zh, sin, zero], axis=1)
    return jnp.stack([c, s1, s2])


def _rope(t, tab_ref, half):
    return (t * tab_ref[0] + pltpu.roll(t, LANES - half, 1) * tab_ref[1]
            + pltpu.roll(t, half, 1) * tab_ref[2])


def _head_norm(t, g):
    return t * lax.rsqrt(jnp.mean(t * t, axis=-1, keepdims=True) + EPS) * g


A_COLS = 4 * WIDTH + LANES


def _layout_w_a(w_in):
    o = 0
    wq = w_in[:, o:o + WIDTH]; o += WIDTH
    wkv = w_in[:, o:o + 2 * A_KV_HEADS * HEAD_DIM]; o += 2 * A_KV_HEADS * HEAD_DIM
    wiq = w_in[:, o:o + IDX_HEADS * IDX_DIM]; o += IDX_HEADS * IDX_DIM
    wik = w_in[:, o:o + IDX_DIM]; o += IDX_DIM
    wiw = w_in[:, o:o + IDX_HEADS]; o += IDX_HEADS
    wz = w_in[:, o:o + WIDTH]
    d = w_in.shape[0]
    wiq = jnp.pad(wiq.reshape(d, IDX_HEADS, IDX_DIM), ((0, 0), (0, 0), (0, LANES - IDX_DIM))).reshape(d, -1)
    wikw = jnp.concatenate([wik, wiw, jnp.zeros((d, LANES - IDX_DIM - IDX_HEADS), w_in.dtype)], axis=1)
    return jnp.concatenate([wq, wkv, wiq, wikw, wz], axis=1).astype(BF16)


def _proj_a_kernel(x_ref, g_ref, w_ref, qg_ref, kg_ref, rt_ref, it_ref,
                   q_out, kv_out, k_out, v_out, iq_out, ikw_out, ikp_out, z_out, xn_ref):
    x = x_ref[...]
    ms = jnp.mean(x * x, axis=-1, keepdims=True)
    xn_ref[...] = (x * lax.rsqrt(ms + EPS) * g_ref[...]).astype(BF16)

    def seg(c0):
        return jnp.dot(xn_ref[...], w_ref[:, c0:c0 + LANES], preferred_element_type=F32)

    qg = qg_ref[...]
    kg = kg_ref[...]
    for h in range(N_HEADS):
        t = _rope(_head_norm(seg(h * LANES), qg), rt_ref, ROT_DIM // 2)
        q_out[:, h * LANES:(h + 1) * LANES] = (t * Q_SCALE).astype(BF16)
    base = WIDTH
    for g in range(A_KV_HEADS):
        k = _rope(_head_norm(seg(base + (2 * g) * LANES), kg), rt_ref, ROT_DIM // 2)
        v = seg(base + (2 * g + 1) * LANES)
        kv_out[:, (2 * g) * LANES:(2 * g + 1) * LANES] = k
        kv_out[:, (2 * g + 1) * LANES:(2 * g + 2) * LANES] = v
        k_out[g] = k.astype(BF16)
        v_out[g] = v.astype(BF16)
    base = 2 * WIDTH
    for h in range(IDX_HEADS):
        t = _rope(seg(base + h * LANES), it_ref, IDX_ROT // 2)
        iq_out[:, h * LANES:(h + 1) * LANES] = t.astype(BF16)
    base = 3 * WIDTH
    t = _rope(seg(base), it_ref, IDX_ROT // 2)
    lane = lax.broadcasted_iota(I32, t.shape, 1)
    w_scale = (IDX_HEADS * IDX_DIM) ** -0.5
    ikw_out[...] = jnp.where(lane < IDX_DIM, t, t * w_scale)
    ikp_out[...] = jnp.where(lane < IDX_DIM, t, 0.0).astype(BF16)
    base = 3 * WIDTH + LANES
    for h in range(N_HEADS):
        z_out[:, h * LANES:(h + 1) * LANES] = seg(base + h * LANES)


def _proj_a(x, norm_g, w_a, q_g, k_g, rope_tab, idx_tab, tm):
    n = x.shape[0]
    period = rope_tab.shape[1] // tm
    row = lambda i: (i, 0)
    const = lambda i: (0, 0)
    tab = lambda i: (0, i % period, 0)
    hm = lambda i: (0, i, 0)
    return pl.pallas_call(
        _proj_a_kernel,
        grid=(n // tm,),
        in_specs=[
            pl.BlockSpec((tm, D_MODEL), row),
            pl.BlockSpec((1, D_MODEL), const),
            pl.BlockSpec((D_MODEL, A_COLS), const),
            pl.BlockSpec((1, HEAD_DIM), const),
            pl.BlockSpec((1, HEAD_DIM), const),
            pl.BlockSpec((3, tm, LANES), tab),
            pl.BlockSpec((3, tm, LANES), tab),
        ],
        out_specs=[
            pl.BlockSpec((tm, WIDTH), row),
            pl.BlockSpec((tm, 2 * A_KV_HEADS * HEAD_DIM), row),
            pl.BlockSpec((A_KV_HEADS, tm, HEAD_DIM), hm),
            pl.BlockSpec((A_KV_HEADS, tm, HEAD_DIM), hm),
            pl.BlockSpec((tm, IDX_HEADS * LANES), row),
            pl.BlockSpec((tm, LANES), row),
            pl.BlockSpec((tm, LANES), row),
            pl.BlockSpec((tm, WIDTH), row),
        ],
        out_shape=[
            jax.ShapeDtypeStruct((n, WIDTH), BF16),
            jax.ShapeDtypeStruct((n, 2 * A_KV_HEADS * HEAD_DIM), F32),
            jax.ShapeDtypeStruct((A_KV_HEADS, n, HEAD_DIM), BF16),
            jax.ShapeDtypeStruct((A_KV_HEADS, n, HEAD_DIM), BF16),
            jax.ShapeDtypeStruct((n, IDX_HEADS * LANES), BF16),
            jax.ShapeDtypeStruct((n, LANES), F32),
            jax.ShapeDtypeStruct((n, LANES), BF16),
            jax.ShapeDtypeStruct((n, WIDTH), F32),
        ],
        scratch_shapes=[pltpu.VMEM((tm, D_MODEL), BF16)],
        compiler_params=_cparams(1),
        name="proj_a",
    )(x, norm_g.reshape(1, -1), w_a, q_g.reshape(1, -1), k_g.reshape(1, -1), rope_tab, idx_tab)


def _out_proj_kernel(x_ref, o_ref, z_ref, w_ref, y_ref):
    z = z_ref[...]
    h = (o_ref[...] * (z * jax.nn.sigmoid(z))).astype(BF16)
    y_ref[...] = x_ref[...] + jnp.dot(h, w_ref[...], preferred_element_type=F32)


def _out_proj(x, o, z, w_o, tm):
    n = x.shape[0]
    row = lambda i: (i, 0)
    return pl.pallas_call(
        _out_proj_kernel,
        grid=(n // tm,),
        in_specs=[pl.BlockSpec((tm, D_MODEL), row), pl.BlockSpec((tm, WIDTH), row),
                  pl.BlockSpec((tm, WIDTH), row), pl.BlockSpec((WIDTH, D_MODEL), lambda i: (0, 0))],
        out_specs=pl.BlockSpec((tm, D_MODEL), row),
        out_shape=jax.ShapeDtypeStruct((n, D_MODEL), F32),
        compiler_params=_cparams(1),
        name="out_proj",
    )(x, o, z, w_o.astype(BF16))


def _key_to_float(t):
    bits = t ^ ((t >> 31) & 0x7FFFFFFF)
    return pltpu.bitcast(bits, F32)


def _neg_zero(x):
    return (x == 0.0) & ((1.0 / x) < 0.0)


def _row_count(sc_ref, n_groups, group, row_chunk, pred, args):
    rows = sc_ref.shape[0]
    rc = min(row_chunk, rows)
    starts = list(range(0, rows, rc))
    parts = []
    for i in range(0, len(starts), 2):
        pair = starts[i:i + 2]
        a = [[x[r0:r0 + rc] if hasattr(x, "shape") and x.shape else x for x in args] for r0 in pair]

        def body(j, cs, pair=pair, a=a):
            cs = list(cs)
            for u in range(group):
                c0 = pl.multiple_of((j * group + u) * LANES, LANES)
                for n, r0 in enumerate(pair):
                    cs[n] = cs[n] + jnp.where(pred(sc_ref[r0:r0 + rc, pl.ds(c0, LANES)], c0, *a[n]), 1, 0)
            return tuple(cs)

        parts.extend(lax.fori_loop(0, n_groups, body, tuple(jnp.zeros((rc, LANES), I32) for _ in pair)))
    c = parts[0] if len(parts) == 1 else jnp.concatenate(parts, axis=0)
    return jnp.broadcast_to(jnp.sum(c, axis=1, keepdims=True), (rows, LANES))


def _topk_threshold(sc_ref, n_groups, group, k, row_chunk=64):
    rows = sc_ref.shape[0]
    count = functools.partial(_row_count, sc_ref, n_groups, group, row_chunk)

    def count_ge(t):
        return count(lambda blk, c0, thr: blk >= thr, [_key_to_float(t)])

    state = (jnp.int32(0), jnp.full((rows, LANES), -2 ** 31, I32), jnp.full((rows, LANES), 2 ** 30, I32))

    def unsettled(st):
        it, _, cnt = st
        return jnp.logical_and(it < 32, jnp.max(jnp.where(cnt != k, 1, 0)) > 0)

    def bit_pair(st):
        it, t, cnt = st
        for step in range(2):
            cand = t + (jnp.int32(1) << (31 - it - step))
            c = count_ge(cand)
            ok = c >= k
            t = jnp.where(ok, cand, t)
            cnt = jnp.where(ok, c, cnt)
        return it + 2, t, cnt

    _, t, n_ge = lax.while_loop(unsettled, bit_pair, state)
    thr = _key_to_float(jnp.maximum(t, KEY_NEG_INF))
    short = thr == -jnp.inf
    excess = jnp.where(short, 0, n_ge - k)
    big = jnp.full((rows, LANES), 2 ** 30, I32)

    def tie_limits():
        need = k - count(lambda blk, c0, thr: blk > thr, [thr])
        n_pos = count(lambda blk, c0, thr: (blk == thr) & ~_neg_zero(blk), [thr])
        need_pos = jnp.minimum(need, n_pos)

        def limit(negative, quota):
            def tie_before(blk, c0, thr, cand):
                col = lax.broadcasted_iota(I32, blk.shape, 1) + c0
                nz = _neg_zero(blk)
                return (blk == thr) & (nz if negative else ~nz) & (col < cand)

            def jbody(it, jl):
                cand = jl + (jnp.int32(1) << (29 - it))
                return jnp.where(count(tie_before, [thr, cand]) <= quota, cand, jl)
            return lax.fori_loop(0, 30, jbody, quota * 0)

        return limit(False, need_pos), limit(True, need - need_pos)

    jpos, jneg = lax.cond(jnp.max(excess) > 0, tie_limits, lambda: (big, big))
    return thr, jnp.where(short, 0, jpos), jnp.where(short, 0, jneg)


def _selected(s, thr, jpos, jneg, col):
    return (s > thr) | ((s == thr) & (col < jnp.where(_neg_zero(s), jneg, jpos)))


def _dsa_prompt_kernel(iq_ref, ikw_ref, ik_ref, q_ref, k_ref, v_ref, o_ref,
                       sc_ref, m_ref, l_ref, acc_ref, *, tq, tk, n_keep):
    i = pl.program_id(1)
    q0 = i * tq
    nkb = lax.div(q0 + tq + tk - 1, tk)
    tiles_per_kb = tk // LANES

    def score_body(j, carry):
        k0 = pl.multiple_of(j * tk, tk)
        ikb = ik_ref[pl.ds(k0, tk), :]
        acc = jnp.zeros((tq, tk), F32)
        for h in range(IDX_HEADS):
            lg = lax.dot_general(iq_ref[:, h * LANES:(h + 1) * LANES], ikb, _NT,
                                 preferred_element_type=F32)
            w = ikw_ref[:, IDX_DIM + h:IDX_DIM + h + 1]
            acc = acc + w * jnp.maximum(lg, 0.0)
        kpos = k0 + lax.broadcasted_iota(I32, (tq, tk), 1)
        tpos = q0 + lax.broadcasted_iota(I32, (tq, tk), 0)
        sc_ref[:, pl.ds(k0, tk)] = jnp.where(kpos <= tpos, acc, -jnp.inf)
        return carry

    lax.fori_loop(0, nkb, score_body, 0)

    thr, jpos, jneg = _topk_threshold(sc_ref, nkb, tiles_per_kb, n_keep)

    half = tq // 2
    lane = lax.broadcasted_iota(I32, (half, LANES), 1)

    def bias_body(j, carry):
        c0 = pl.multiple_of(j * LANES, LANES)
        for r0 in (0, half):
            sel = _selected(sc_ref[r0:r0 + half, pl.ds(c0, LANES)], thr[r0:r0 + half], jpos[r0:r0 + half],
                            jneg[r0:r0 + half], lane + c0)
            sc_ref[r0:r0 + half, pl.ds(c0, LANES)] = _mask_bias(sel)
        return carry

    lax.fori_loop(0, nkb * tiles_per_kb, bias_body, 0)

    _flash_init(m_ref, l_ref, acc_ref)

    def attn_body(j, carry):
        k0 = pl.multiple_of(j * tk, tk)
        bias = [sc_ref[:, pl.ds(k0 + t * LANES, LANES)] for t in range(tiles_per_kb)]
        for h in range(N_HEADS):
            g = h // (N_HEADS // A_KV_HEADS)
            _flash_step(q_ref[:, h * LANES:(h + 1) * LANES], k_ref[g, pl.ds(k0, tk), :],
                        v_ref[g, pl.ds(k0, tk), :], bias, m_ref, l_ref, acc_ref, h)
        return carry

    lax.fori_loop(0, nkb, attn_body, 0)
    for h in range(N_HEADS):
        o_ref[:, h * LANES:(h + 1) * LANES] = acc_ref[h] / l_ref[h]


def _dsa_prompt(iq, ikw, ikp, q, k_hm, v_hm, batch, seq, n_keep, tq=512, tk=512):
    tq = min(tq, seq)
    tk = min(tk, seq)
    nq = seq // tq
    n = batch * seq
    row = lambda b, i: (b * nq + i, 0)
    kern = functools.partial(_dsa_prompt_kernel, tq=tq, tk=tk, n_keep=n_keep)
    once = pl.Buffered(1)
    return pl.pallas_call(
        kern,
        grid=(batch, nq),
        in_specs=[
            pl.BlockSpec((tq, IDX_HEADS * LANES), row, pipeline_mode=once),
            pl.BlockSpec((tq, LANES), row),
            pl.BlockSpec((seq, LANES), lambda b, i: (b, 0), pipeline_mode=once),
            pl.BlockSpec((tq, WIDTH), row, pipeline_mode=once),
            pl.BlockSpec((A_KV_HEADS, seq, HEAD_DIM), lambda b, i: (0, b, 0), pipeline_mode=once),
            pl.BlockSpec((A_KV_HEADS, seq, HEAD_DIM), lambda b, i: (0, b, 0), pipeline_mode=once),
        ],
        out_specs=pl.BlockSpec((tq, WIDTH), row),
        out_shape=jax.ShapeDtypeStruct((n, WIDTH), F32),
        scratch_shapes=[
            pltpu.VMEM((tq, seq + LANES), F32),
            pltpu.VMEM((N_HEADS, tq, LANES), F32),
            pltpu.VMEM((N_HEADS, tq, LANES), F32),
            pltpu.VMEM((N_HEADS, tq, HEAD_DIM), F32),
        ],
        compiler_params=_cparams(2),
        name="dsa_prompt",
    )(iq, ikw, ikp, q, k_hm, v_hm)


def _dsa_prompt_layer(x, norm_g, w_in, q_g, k_g, w_o):
    b, s, _ = x.shape
    n = b * s
    pos = jnp.arange(s)
    tm = min(256, s)
    xf = x.reshape(n, D_MODEL)
    q, kv, k_hm, v_hm, iq, ikw, ikp, z = _proj_a(
        xf, norm_g, _layout_w_a(w_in), q_g, k_g, _rope_tables(pos, ROT_DIM), _rope_tables(pos, IDX_ROT), tm)
    o = _dsa_prompt(iq, ikw, ikp, q, k_hm, v_hm, b, s, min(TOPK_MAX, s // 4))
    y = _out_proj(xf, o, z, w_o, tm)
    return (y.reshape(b, s, D_MODEL), kv.reshape(b, s, A_KV_HEADS, 2, HEAD_DIM),
            ikw[:, :IDX_DIM].reshape(b, s, IDX_DIM))


B_KVW = 2 * B_KV_HEADS * HEAD_DIM
B_COLS = 2 * WIDTH + 3 * B_KVW + LANES


def _layout_w_b(w_in):
    o = WIDTH + 3 * B_KVW
    d = w_in.shape[0]
    wg = jnp.concatenate([w_in[:, o:o + 3 * N_HEADS], jnp.zeros((d, LANES - 3 * N_HEADS), w_in.dtype)], axis=1)
    return jnp.concatenate([w_in[:, :o], wg, w_in[:, o + 3 * N_HEADS:]], axis=1).astype(BF16)


def _proj_b_kernel(x_ref, g_ref, w_ref, qg_ref, kg_ref, rt_ref,
                   q_out, kvc_out, kvs_out, kvw_out, ks_out, vs_out, kw_out, vw_out, gate_out, z_out, xn_ref):
    x = x_ref[...]
    ms = jnp.mean(x * x, axis=-1, keepdims=True)
    xn_ref[...] = (x * lax.rsqrt(ms + EPS) * g_ref[...]).astype(BF16)

    def seg(c0):
        return jnp.dot(xn_ref[...], w_ref[:, c0:c0 + LANES], preferred_element_type=F32)

    qg = qg_ref[...]
    for h in range(N_HEADS):
        t = _rope(_head_norm(seg(h * LANES), qg), rt_ref, ROT_DIM // 2)
        q_out[:, h * LANES:(h + 1) * LANES] = (t * Q_SCALE).astype(BF16)
    branch_out = ((kvc_out, None, None), (kvs_out, ks_out, vs_out), (kvw_out, kw_out, vw_out))
    for br, (kv_out, k_out, v_out) in enumerate(branch_out):
        base = WIDTH + br * B_KVW
        kg = kg_ref[br:br + 1, :]
        for g in range(B_KV_HEADS):
            k = _rope(_head_norm(seg(base + (2 * g) * LANES), kg), rt_ref, ROT_DIM // 2)
            v = seg(base + (2 * g + 1) * LANES)
            kv_out[:, (2 * g) * LANES:(2 * g + 1) * LANES] = k
            kv_out[:, (2 * g + 1) * LANES:(2 * g + 2) * LANES] = v
            if k_out is not None:
                k_out[g] = k.astype(BF16)
                v_out[g] = v.astype(BF16)
    base = WIDTH + 3 * B_KVW
    gate_out[...] = jax.nn.sigmoid(seg(base))
    base += LANES
    for h in range(N_HEADS):
        z_out[:, h * LANES:(h + 1) * LANES] = seg(base + h * LANES)


def _proj_b(x, norm_g, w_b, q_g, k_g, rope_tab, tm):
    n = x.shape[0]
    period = rope_tab.shape[1] // tm
    row = lambda i: (i, 0)
    const = lambda i: (0, 0)
    hm = lambda i: (0, i, 0)
    kv_spec = pl.BlockSpec((tm, B_KVW), row)
    hm_spec = pl.BlockSpec((B_KV_HEADS, tm, HEAD_DIM), hm)
    kv_shape = jax.ShapeDtypeStruct((n, B_KVW), F32)
    hm_shape = jax.ShapeDtypeStruct((B_KV_HEADS, n, HEAD_DIM), BF16)
    return pl.pallas_call(
        _proj_b_kernel,
        grid=(n // tm,),
        in_specs=[
            pl.BlockSpec((tm, D_MODEL), row),
            pl.BlockSpec((1, D_MODEL), const),
            pl.BlockSpec((D_MODEL, B_COLS), const),
            pl.BlockSpec((1, HEAD_DIM), const),
            pl.BlockSpec((3, HEAD_DIM), const),
            pl.BlockSpec((3, tm, LANES), lambda i: (0, i % period, 0)),
        ],
        out_specs=[pl.BlockSpec((tm, WIDTH), row), kv_spec, kv_spec, kv_spec,
                   hm_spec, hm_spec, hm_spec, hm_spec,
                   pl.BlockSpec((tm, LANES), row), pl.BlockSpec((tm, WIDTH), row)],
        out_shape=[jax.ShapeDtypeStruct((n, WIDTH), BF16), kv_shape, kv_shape, kv_shape,
                   hm_shape, hm_shape, hm_shape, hm_shape,
                   jax.ShapeDtypeStruct((n, LANES), F32), jax.ShapeDtypeStruct((n, WIDTH), F32)],
        scratch_shapes=[pltpu.VMEM((tm, D_MODEL), BF16)],
        compiler_params=_cparams(1),
        name="proj_b",
    )(x, norm_g.reshape(1, -1), w_b, q_g.reshape(1, -1), k_g, rope_tab)


CHUNK_COLS = CMP_STRIDE * B_KVW


def _layout_cmp(w_cmp, pe_cmp):
    r = CMP_BLOCK // CMP_STRIDE
    eye_g = jnp.eye(B_KV_HEADS, dtype=w_cmp.dtype)
    eye_c = jnp.eye(2, dtype=w_cmp.dtype)
    wj = w_cmp.reshape(r, CMP_STRIDE, 2, HEAD_DIM, HEAD_DIM)
    w = jnp.einsum('jlcde,gh,ck->jlgcdhke', wj, eye_g, eye_c).reshape(r, CHUNK_COLS, B_KVW)
    pe = jnp.broadcast_to(pe_cmp.reshape(r, CMP_STRIDE, 1, 2, HEAD_DIM),
                          (r, CMP_STRIDE, B_KV_HEADS, 2, HEAD_DIM)).reshape(r, 1, CHUNK_COLS)
    return w.astype(BF16), pe


def _cmp_mm_kernel(x_ref, pe_ref, w_ref, p0_ref, p1_ref):
    x = x_ref[...]
    p0_ref[...] = jnp.dot((x + pe_ref[0]).astype(BF16), w_ref[0], preferred_element_type=F32)
    p1_ref[...] = jnp.dot((x + pe_ref[1]).astype(BF16), w_ref[1], preferred_element_type=F32)


def _cmp_mm(x, w, pe, tr):
    r = x.shape[0]
    row = lambda i: (i, 0)
    once = pl.Buffered(1)
    return pl.pallas_call(
        _cmp_mm_kernel,
        grid=(r // tr,),
        in_specs=[pl.BlockSpec((tr, CHUNK_COLS), row),
                  pl.BlockSpec((2, 1, CHUNK_COLS), lambda i: (0, 0, 0)),
                  pl.BlockSpec((2, CHUNK_COLS, B_KVW), lambda i: (0, 0, 0), pipeline_mode=once)],
        out_specs=[pl.BlockSpec((tr, B_KVW), row), pl.BlockSpec((tr, B_KVW), row)],
        out_shape=[jax.ShapeDtypeStruct((r, B_KVW), F32), jax.ShapeDtypeStruct((r, B_KVW), F32)],
        compiler_params=_cparams(1),
        name="cmp_mm",
    )(x, pe, w)


def _overlap_matrix(n_rows, n_cmp, n_cols):
    i = np.arange(n_rows)[:, None]
    j = np.arange(n_cols)[None, :]
    lo = np.maximum(i * CMP_STRIDE, j * SLC_BLOCK)
    hi = np.minimum(i * CMP_STRIDE + CMP_BLOCK, (j + 1) * SLC_BLOCK)
    ov = np.maximum(hi - lo, 0) / CMP_STRIDE
    ov = np.where(i < n_cmp, ov, 0.0)
    return jnp.asarray(ov, dtype=BF16)


def _dot_f32(a, b_bf16):
    hi = a.astype(BF16)
    r1 = a - hi.astype(F32)
    mid = r1.astype(BF16)
    lo = (r1 - mid.astype(F32)).astype(BF16)
    d = lambda p: jnp.dot(p, b_bf16, preferred_element_type=F32)
    return d(hi) + d(mid) + d(lo)


def _mask_bias(valid):
    return jnp.where(valid, 0.0, NEG_BIG)


def _flash_step(q, k, v, bias_tiles, m_ref, l_ref, acc_ref, h):
    s = lax.dot_general(q, k, _NT, preferred_element_type=F32)
    st = [s[:, t * LANES:(t + 1) * LANES] + b for t, b in enumerate(bias_tiles)]
    rows = s.shape[0]
    mt = functools.reduce(jnp.maximum, st)
    m_prev = m_ref[h]
    m_new = jnp.maximum(m_prev, jnp.broadcast_to(jnp.max(mt, axis=1, keepdims=True), (rows, LANES)))
    alpha = jnp.exp2(m_prev - m_new)
    pt = [jnp.exp2(x - m_new) for x in st]
    lt = functools.reduce(jnp.add, pt)
    l_ref[h] = alpha * l_ref[h] + jnp.broadcast_to(jnp.sum(lt, axis=1, keepdims=True), (rows, LANES))
    p = pt[0] if len(pt) == 1 else jnp.concatenate(pt, axis=1)
    acc_ref[h] = alpha * acc_ref[h] + jnp.dot(p.astype(BF16), v, preferred_element_type=F32)
    m_ref[h] = m_new


def _flash_init(m_ref, l_ref, acc_ref):
    m_ref[...] = jnp.full(m_ref.shape, NEG_BIG, F32)
    l_ref[...] = jnp.zeros(l_ref.shape, F32)
    acc_ref[...] = jnp.zeros(acc_ref.shape, F32)


def _topk_cols(xt_ref, sel_ref, k):
    nb, c = xt_ref.shape
    slabs = [(r0, xt_ref[r0:r0 + 8, :]) for r0 in range(0, nb, 8)]
    row = lax.broadcasted_iota(I32, (8, c), 0)

    def count(pred):
        acc = jnp.zeros((8, c), I32)
        for r0, blk in slabs:
            acc = acc + jnp.where(pred(blk, r0), 1, 0)
        return jnp.broadcast_to(jnp.sum(acc, axis=0, keepdims=True), (8, c))

    def bit_body(it, t):
        cand = t + (jnp.int32(1) << (31 - it))
        cand_f = _key_to_float(cand)
        return jnp.where(count(lambda blk, r0: blk >= cand_f) >= k, cand, t)

    zero = jnp.zeros((8, c), I32)
    t = lax.fori_loop(0, 32, bit_body, jnp.full((8, c), -2 ** 31, I32))
    thr = _key_to_float(jnp.maximum(t, KEY_NEG_INF))
    short = thr == -jnp.inf
    n_gt = count(lambda blk, r0: blk > thr)
    excess = jnp.where(short, 0, count(lambda blk, r0: blk >= thr) - k)
    need = k - n_gt
    n_bits = max(1, int(nb).bit_length())

    def tie_limit():
        jl = zero
        for it in range(n_bits):
            cand = jl + (1 << (n_bits - 1 - it))
            cnt = count(lambda blk, r0: (blk == thr) & ((row + r0) < cand))
            jl = jnp.where(cnt <= need, cand, jl)
        return jl

    jlim = lax.cond(jnp.max(excess) > 0, tie_limit, lambda: jnp.full((8, c), 2 ** 30, I32))
    jlim = jnp.where(short, 0, jlim)
    for r0, blk in slabs:
        sel_ref[r0:r0 + 8, :] = jnp.where((blk > thr) | ((blk == thr) & ((row + r0) < jlim)), 1.0, 0.0)


def _nsa_prompt_kernel(q_ref, gate_ref, p0_ref, p1_ref, ov_ref, ks_ref, vs_ref, kw_ref, vw_ref, o_ref,
                       kc_ref, vc_ref, impt_ref, selt_ref, oc_ref, os_ref, m_ref, l_ref, acc_ref,
                       *, tq, tk, wk, n_cmp, n_sel):
    i = pl.program_id(1)
    q0 = i * tq
    nc = p0_ref.shape[0]

    @pl.when(i == 0)
    def _():
        kc = p0_ref[...] + pltpu.roll(p1_ref[...], nc - 1, 0)
        for g in range(B_KV_HEADS):
            kc_ref[g] = kc[:, (2 * g) * LANES:(2 * g + 1) * LANES].astype(BF16)
            vc_ref[g] = kc[:, (2 * g + 1) * LANES:(2 * g + 2) * LANES].astype(BF16)

    tpos_c = q0 + lax.broadcasted_iota(I32, (tq, nc), 0)
    nidx = lax.broadcasted_iota(I32, (tq, nc), 1)
    cvalid = jnp.where(nidx < n_cmp, nidx * CMP_STRIDE + (CMP_BLOCK - 1), 2 ** 30) <= tpos_c
    tpos = q0 + lax.broadcasted_iota(I32, (tq, LANES), 0)
    jblk = lax.broadcasted_iota(I32, (tq, LANES), 1)
    cur = tpos >> 6
    forced = (jblk == 0) | (jblk == cur) | (jblk == cur - 1)
    admissible = jblk * SLC_BLOCK <= tpos
    sel = []
    for g in range(B_KV_HEADS):
        pg = jnp.zeros((tq, nc), F32)
        for r in range(GB):
            h = g * GB + r
            s = lax.dot_general(q_ref[:, h * LANES:(h + 1) * LANES], kc_ref[g], _NT,
                                preferred_element_type=F32)
            s = jnp.where(cvalid, s, NEG_BIG)
            e = jnp.where(cvalid, jnp.exp2(s - jnp.max(s, axis=1, keepdims=True)), 0.0)
            l = jnp.sum(e, axis=1, keepdims=True)
            p = e / jnp.where(l > 0.0, l, 1.0)
            oc_ref[h] = jnp.dot(p.astype(BF16), vc_ref[g], preferred_element_type=F32)
            pg = pg + p
        imp = _dot_f32(pg, ov_ref[...])
        imp = jnp.where(forced, jnp.inf, jnp.where(admissible, imp, -jnp.inf))
        impt_ref[...] = imp.T
        _topk_cols(impt_ref, selt_ref, n_sel)
        sel.append(selt_ref[...].T.astype(BF16))

    nkb = lax.div(q0 + tq + tk - 1, tk)
    _flash_init(m_ref, l_ref, acc_ref)
    blk_row = lax.broadcasted_iota(I32, (LANES, tk), 0)
    key_col = lax.broadcasted_iota(I32, (LANES, tk), 1)
    tpos_k = q0 + lax.broadcasted_iota(I32, (tq, tk), 0)
    kcol = lax.broadcasted_iota(I32, (tq, tk), 1)

    def slc_body(j, carry):
        k0 = pl.multiple_of(j * tk, tk)
        expand = jnp.where(((key_col + k0) >> 6) == blk_row, 1.0, 0.0).astype(BF16)
        causal = (kcol + k0) <= tpos_k
        for g in range(B_KV_HEADS):
            hit = jnp.dot(sel[g], expand, preferred_element_type=F32)
            bias = _mask_bias(jnp.where(causal, hit, 0.0) > 0.5)
            bias = [bias[:, t * LANES:(t + 1) * LANES] for t in range(tk // LANES)]
            for r in range(GB):
                h = g * GB + r
                _flash_step(q_ref[:, h * LANES:(h + 1) * LANES], ks_ref[g, pl.ds(k0, tk), :],
                            vs_ref[g, pl.ds(k0, tk), :], bias, m_ref, l_ref, acc_ref, h)
        return carry

    lax.fori_loop(0, nkb, slc_body, 0)
    for h in range(N_HEADS):
        os_ref[h] = acc_ref[h] / l_ref[h]

    w0 = pl.multiple_of(jnp.maximum(q0 - WINDOW, 0), tq)
    kpos = w0 + lax.broadcasted_iota(I32, (tq, wk), 1)
    tpos_w = q0 + lax.broadcasted_iota(I32, (tq, wk), 0)
    bias_w = _mask_bias(jnp.where(kpos <= tpos_w, kpos, -2 ** 30) > tpos_w - WINDOW)
    for h in range(N_HEADS):
        g = h // GB
        s = lax.dot_general(q_ref[:, h * LANES:(h + 1) * LANES], kw_ref[g, pl.ds(w0, wk), :], _NT,
                            preferred_element_type=F32) + bias_w
        p = jnp.exp2(s - jnp.max(s, axis=1, keepdims=True))
        ow = (jnp.dot(p.astype(BF16), vw_ref[g, pl.ds(w0, wk), :], preferred_element_type=F32)
              / jnp.sum(p, axis=1, keepdims=True))
        o_ref[:, h * LANES:(h + 1) * LANES] = (gate_ref[:, 3 * h:3 * h + 1] * oc_ref[h]
                                               + gate_ref[:, 3 * h + 1:3 * h + 2] * os_ref[h]
                                               + gate_ref[:, 3 * h + 2:3 * h + 3] * ow)


def _nsa_prompt(q, gates, p0, p1, ks, vs, kw, vw, batch, seq, tq=512, tk=512):
    tq = min(tq, seq)
    tk = min(tk, seq)
    nq = seq // tq
    n = batch * seq
    nc = seq // CMP_STRIDE
    n_cmp = nc - CMP_BLOCK // CMP_STRIDE + 1
    n_slc = seq // SLC_BLOCK
    assert n_slc <= LANES and WINDOW % tq == 0
    ov = _overlap_matrix(nc, n_cmp, LANES)
    row = lambda b, i: (b * nq + i, 0)
    per_b = lambda b, i: (b, 0)
    hm_b = lambda b, i: (0, b, 0)
    once = pl.Buffered(1)
    kern = functools.partial(_nsa_prompt_kernel, tq=tq, tk=tk, wk=min(WINDOW + tq, seq), n_cmp=n_cmp,
                             n_sel=min(SLC_TOPN, n_slc))
    hm_spec = pl.BlockSpec((B_KV_HEADS, seq, HEAD_DIM), hm_b, pipeline_mode=once)
    return pl.pallas_call(
        kern,
        grid=(batch, nq),
        in_specs=[
            pl.BlockSpec((tq, WIDTH), row),
            pl.BlockSpec((tq, LANES), row),
            pl.BlockSpec((nc, B_KVW), per_b, pipeline_mode=once),
            pl.BlockSpec((nc, B_KVW), per_b, pipeline_mode=once),
            pl.BlockSpec((nc, LANES), lambda b, i: (0, 0), pipeline_mode=once),
            hm_spec, hm_spec, hm_spec, hm_spec,
        ],
        out_specs=pl.BlockSpec((tq, WIDTH), row),
        out_shape=jax.ShapeDtypeStruct((n, WIDTH), F32),
        scratch_shapes=[
            pltpu.VMEM((B_KV_HEADS, nc, HEAD_DIM), BF16),
            pltpu.VMEM((B_KV_HEADS, nc, HEAD_DIM), BF16),
            pltpu.VMEM((LANES, tq), F32),
            pltpu.VMEM((LANES, tq), F32),
            pltpu.VMEM((N_HEADS, tq, HEAD_DIM), F32),
            pltpu.VMEM((N_HEADS, tq, HEAD_DIM), F32),
            pltpu.VMEM((N_HEADS, tq, LANES), F32),
            pltpu.VMEM((N_HEADS, tq, LANES), F32),
            pltpu.VMEM((N_HEADS, tq, HEAD_DIM), F32),
        ],
        compiler_params=_cparams(2),
        name="nsa_prompt",
    )(q, gates, p0, p1, ov, ks, vs, kw, vw)


def _nsa_prompt_layer(x, norm_g, w_in, q_g, k_g, w_cmp, pe_cmp, w_o):
    b, s, _ = x.shape
    n = b * s
    tm = min(256, s)
    xf = x.reshape(n, D_MODEL)
    q, kvc, kvs, kvw, ks, vs, kw, vw, gates, z = _proj_b(
        xf, norm_g, _layout_w_b(w_in), q_g, k_g, _rope_tables(jnp.arange(s), ROT_DIM), tm)
    wc, pe = _layout_cmp(w_cmp, pe_cmp)
    nc = s // CMP_STRIDE
    p0, p1 = _cmp_mm(kvc.reshape(b * nc, CHUNK_COLS), wc, pe, min(128, nc))
    o = _nsa_prompt(q, gates, p0, p1, ks, vs, kw, vw, b, s)
    y = _out_proj(xf, o, z, w_o, tm)
    shp = (b, s, B_KV_HEADS, 2, HEAD_DIM)
    wlen = min(WINDOW, s)
    return (y.reshape(b, s, D_MODEL), kvc.reshape(shp), kvs.reshape(shp),
            kvw.reshape(shp)[:, s - wlen:])


def _page_specs(block, pages_per_step):
    return [pl.BlockSpec((None,) + block,
                         lambda b, j, pt, k=k: (pt[b, j * pages_per_step + k],) + (0,) * len(block))
            for k in range(pages_per_step)]


def _split_tiles(n_tiles, max_group=16):
    group = max(g for g in range(1, max_group + 1) if n_tiles % g == 0)
    return n_tiles // group, group


def _row_token(shape):
    return lax.broadcasted_iota(I32, shape, 0) & 3


def _dsa_sample_score_kernel(pt_ref, iq_ref, iw_ref, ikn_ref, *rest, pps, past_len, n_keep):
    pages = rest[:pps]
    sc_ref, thr_ref, jlim_ref = rest[pps:]
    j = pl.program_id(1)
    iq = iq_ref[:, :IDX_DIM]

    def score(keys_bf16):
        lg = lax.dot_general(iq, keys_bf16, _NT, preferred_element_type=F32)
        acc = jnp.zeros((8, LANES), F32)
        for h in range(IDX_HEADS):
            acc = acc + iw_ref[h * 8:(h + 1) * 8, :] * jnp.maximum(lg[h * 8:(h + 1) * 8, :], 0.0)
        return acc

    for k in range(pps):
        c0 = pl.multiple_of((j * pps + k) * PAGE_SIZE, PAGE_SIZE)
        sc_ref[:, pl.ds(c0, PAGE_SIZE)] = score(pages[k][...].astype(BF16))

    @pl.when(j == pl.num_programs(1) - 1)
    def _():
        s_new = score(ikn_ref[:, :IDX_DIM])
        col = lax.broadcasted_iota(I32, (8, LANES), 1)
        sc_ref[:, past_len:past_len + LANES] = jnp.where(col <= _row_token((8, LANES)), s_new, -jnp.inf)
        n_groups, group = _split_tiles(sc_ref.shape[1] // LANES)
        thr, jpos, jneg = _topk_threshold(sc_ref, n_groups, group, n_keep)
        thr_ref[...] = thr
        jlim_ref[0] = jpos
        jlim_ref[1] = jneg


def _dsa_sample_scores(page_table, cache_idx, iq_s, iw_s, ik_new, n_keep, pps=8):
    nb, n_pages = page_table.shape
    past_len = n_pages * PAGE_SIZE
    scw = past_len + LANES
    per_b = lambda b, j, pt: (b, 0, 0)
    kern = functools.partial(_dsa_sample_score_kernel, pps=pps, past_len=past_len, n_keep=n_keep)
    grid_spec = pltpu.PrefetchScalarGridSpec(
        num_scalar_prefetch=1,
        grid=(nb, n_pages // pps),
        in_specs=[pl.BlockSpec((None, 64, LANES), per_b),
                  pl.BlockSpec((None, 64, LANES), per_b),
                  pl.BlockSpec((None, LANES, LANES), per_b)] + _page_specs((PAGE_SIZE, IDX_DIM), pps),
        out_specs=[pl.BlockSpec((None, 8, scw), per_b),
                   pl.BlockSpec((None, 8, LANES), per_b),
                   pl.BlockSpec((None, 2, 8, LANES), lambda b, j, pt: (b, 0, 0, 0))],
    )
    return pl.pallas_call(
        kern,
        grid_spec=grid_spec,
        out_shape=[jax.ShapeDtypeStruct((nb, 8, scw), F32),
                   jax.ShapeDtypeStruct((nb, 8, LANES), F32),
                   jax.ShapeDtypeStruct((nb, 2, 8, LANES), I32)],
        compiler_params=_cparams(2),
        name="dsa_sample_scores",
    )(page_table, iq_s, iw_s, ik_new, *([cache_idx] * pps))


def _dsa_sample_attn_kernel(pt_ref, q_ref, sc_ref, thr_ref, jlim_ref, kn_ref, vn_ref, *rest, pps, past_len):
    pages = rest[:pps]
    o_ref, kbuf_ref, vbuf_ref, m_ref, l_ref, acc_ref = rest[pps:]
    j = pl.program_id(1)
    thr = thr_ref[...]
    jpos = jlim_ref[0]
    jneg = jlim_ref[1]
    lane = lax.broadcasted_iota(I32, (8, LANES), 1)
    rows_per_token = 2 * A_KV_HEADS

    @pl.when(j == 0)
    def _():
        _flash_init(m_ref, l_ref, acc_ref)

    def bias_at(c0):
        return _mask_bias(_selected(sc_ref[:, pl.ds(c0, LANES)], thr, jpos, jneg, lane + c0))

    for k in range(pps):
        for g in range(A_KV_HEADS):
            rows = pl.ds(k * PAGE_SIZE, PAGE_SIZE)
            kbuf_ref[g, rows, :] = pages[k][pl.ds(2 * g, PAGE_SIZE, stride=rows_per_token), :].astype(BF16)
            vbuf_ref[g, rows, :] = pages[k][pl.ds(2 * g + 1, PAGE_SIZE, stride=rows_per_token), :].astype(BF16)
    bias = [bias_at(pl.multiple_of((j * pps + k) * PAGE_SIZE, PAGE_SIZE)) for k in range(pps)]
    for g in range(A_KV_HEADS):
        _flash_step(q_ref[g], kbuf_ref[g], vbuf_ref[g], bias, m_ref, l_ref, acc_ref, g)

    @pl.when(j == pl.num_programs(1) - 1)
    def _():
        bias_new = [bias_at(past_len)]
        for g in range(A_KV_HEADS):
            _flash_step(q_ref[g], kn_ref[g], vn_ref[g], bias_new, m_ref, l_ref, acc_ref, g)
            o_ref[g] = acc_ref[g] / l_ref[g]


def _dsa_sample_attn(page_table, cache_kv, q_s, scores, thr, jlim, k_new, v_new, pps=8):
    nb, n_pages = page_table.shape
    pps = min(pps, n_pages)
    past_len = n_pages * PAGE_SIZE
    scw = scores.shape[-1]
    per_b3 = lambda b, j, pt: (b, 0, 0)
    per_b4 = lambda b, j, pt: (b, 0, 0, 0)
    page_rows = PAGE_SIZE * 2 * A_KV_HEADS
    kern = functools.partial(_dsa_sample_attn_kernel, pps=pps, past_len=past_len)
    grid_spec = pltpu.PrefetchScalarGridSpec(
        num_scalar_prefetch=1,
        grid=(nb, n_pages // pps),
        in_specs=[pl.BlockSpec((None, A_KV_HEADS, 8, HEAD_DIM), per_b4),
                  pl.BlockSpec((None, 8, scw), per_b3),
                  pl.BlockSpec((None, 8, LANES), per_b3),
                  pl.BlockSpec((None, 2, 8, LANES), per_b4),
                  pl.BlockSpec((None, A_KV_HEADS, LANES, HEAD_DIM), per_b4),
                  pl.BlockSpec((None, A_KV_HEADS, LANES, HEAD_DIM), per_b4)]
        + _page_specs((page_rows, HEAD_DIM), pps),
        out_specs=pl.BlockSpec((None, A_KV_HEADS, 8, HEAD_DIM), per_b4),
        scratch_shapes=[pltpu.VMEM((A_KV_HEADS, pps * PAGE_SIZE, HEAD_DIM), BF16),
                        pltpu.VMEM((A_KV_HEADS, pps * PAGE_SIZE, HEAD_DIM), BF16),
                        pltpu.VMEM((A_KV_HEADS, 8, LANES), F32),
                        pltpu.VMEM((A_KV_HEADS, 8, LANES), F32),
                        pltpu.VMEM((A_KV_HEADS, 8, HEAD_DIM), F32)],
    )
    return pl.pallas_call(
        kern,
        grid_spec=grid_spec,
        out_shape=jax.ShapeDtypeStruct((nb, A_KV_HEADS, 8, HEAD_DIM), F32),
        compiler_params=_cparams(2),
        name="dsa_sample_attn",
    )(page_table, q_s, scores, thr, jlim, k_new, v_new,
      *([cache_kv.reshape(cache_kv.shape[0], page_rows, HEAD_DIM)] * pps))


def _pad_keys(a, axis):
    pad = [(0, 0)] * a.ndim
    pad[axis] = (0, LANES - a.shape[axis])
    return jnp.pad(a, pad)


def _dsa_sample_layer(x, cache_kv, cache_idx, page_table, norm_g, w_in, q_g, k_g, w_o):
    nb, nt, _ = x.shape
    assert nt == 4
    n = nb * nt
    past_len = page_table.shape[1] * PAGE_SIZE
    pos = jnp.tile(past_len + jnp.arange(nt), nb)
    xf = x.reshape(n, D_MODEL)
    q, kv, k_hm, v_hm, iq, ikw, ikp, z = _proj_a(
        xf, norm_g, _layout_w_a(w_in), q_g, k_g, _rope_tables(pos, ROT_DIM), _rope_tables(pos, IDX_ROT), n)
    ga = N_HEADS // A_KV_HEADS
    iq_s = iq.reshape(nb, nt, IDX_HEADS, LANES).transpose(0, 2, 1, 3)
    iq_s = jnp.concatenate([iq_s, iq_s], axis=2).reshape(nb, 64, LANES)
    iw = ikw[:, IDX_DIM:IDX_DIM + IDX_HEADS].reshape(nb, nt, IDX_HEADS).transpose(0, 2, 1)
    iw_s = jnp.broadcast_to(jnp.concatenate([iw, iw], axis=2).reshape(nb, 64, 1), (nb, 64, LANES))
    ik_new = _pad_keys(ikp.reshape(nb, nt, LANES), 1)
    q_s = q.reshape(nb, nt, A_KV_HEADS, ga, HEAD_DIM).transpose(0, 2, 3, 1, 4).reshape(nb, A_KV_HEADS, 8, HEAD_DIM)
    k_new = _pad_keys(k_hm.reshape(A_KV_HEADS, nb, nt, HEAD_DIM).transpose(1, 0, 2, 3), 2)
    v_new = _pad_keys(v_hm.reshape(A_KV_HEADS, nb, nt, HEAD_DIM).transpose(1, 0, 2, 3), 2)
    n_keep = min(TOPK_MAX, (past_len + nt) // 4)
    scores, thr, jlim = _dsa_sample_scores(page_table, cache_idx, iq_s, iw_s, ik_new, n_keep)
    o_s = _dsa_sample_attn(page_table, cache_kv, q_s, scores, thr, jlim, k_new, v_new)
    o = o_s.reshape(nb, A_KV_HEADS, ga, nt, HEAD_DIM).transpose(0, 3, 1, 2, 4).reshape(n, WIDTH)
    y = _out_proj(xf, o, z, w_o, n)
    return (y.reshape(nb, nt, D_MODEL), kv.reshape(nb, nt, A_KV_HEADS, 2, HEAD_DIM),
            ikw[:, :IDX_DIM].reshape(nb, nt, IDX_DIM))


CHUNKS_PER_PAGE = PAGE_SIZE // CMP_STRIDE


SLABS_B = 2 * B_KV_HEADS
SLAB_COLS = CMP_STRIDE * HEAD_DIM


def _layout_cmp_slabs(w_cmp, pe_cmp):
    r = CMP_BLOCK // CMP_STRIDE
    w = w_cmp.reshape(r, CMP_STRIDE, 2, HEAD_DIM, HEAD_DIM).transpose(0, 2, 1, 3, 4)
    pe = pe_cmp.reshape(r, CMP_STRIDE, 2, HEAD_DIM).transpose(0, 2, 1, 3)
    return w.reshape(r, 2, SLAB_COLS, HEAD_DIM).astype(BF16), pe.reshape(r, 2, 1, SLAB_COLS)


def _cmp_paged_kernel(pt_ref, xn_ref, pe_ref, w_ref, *rest, pps):
    pages = rest[:pps]
    p0_ref, p1_ref, pn_ref, xs_ref = rest[pps:]
    for k in range(pps):
        for l in range(CMP_STRIDE):
            for s in range(SLABS_B):
                xs_ref[s, k * CHUNKS_PER_PAGE:(k + 1) * CHUNKS_PER_PAGE, l * HEAD_DIM:(l + 1) * HEAD_DIM] = (
                    pages[k][pl.ds(l * SLABS_B + s, CHUNKS_PER_PAGE, stride=CMP_STRIDE * SLABS_B), :])
    for s in range(SLABS_B):
        c = s % 2
        x = xs_ref[s]
        cols = slice(s * HEAD_DIM, (s + 1) * HEAD_DIM)
        p0_ref[:, cols] = jnp.dot((x + pe_ref[0, c]).astype(BF16), w_ref[0, c], preferred_element_type=F32)
        p1_ref[:, cols] = jnp.dot((x + pe_ref[1, c]).astype(BF16), w_ref[1, c], preferred_element_type=F32)
        pn_ref[:, cols] = jnp.dot((xn_ref[s] + pe_ref[1, c]).astype(BF16), w_ref[1, c],
                                  preferred_element_type=F32)


def _cmp_paged(page_table, cache_cmp, x_new, w, pe, pps=32):
    nb, n_pages = page_table.shape
    pps = min(pps, n_pages)
    steps = n_pages // pps
    rows = pps * CHUNKS_PER_PAGE
    kern = functools.partial(_cmp_paged_kernel, pps=pps)
    grid_spec = pltpu.PrefetchScalarGridSpec(
        num_scalar_prefetch=1,
        grid=(nb, steps),
        in_specs=[pl.BlockSpec((None, SLABS_B, 8, SLAB_COLS), lambda b, j, pt: (b, 0, 0, 0)),
                  pl.BlockSpec((2, 2, 1, SLAB_COLS), lambda b, j, pt: (0, 0, 0, 0)),
                  pl.BlockSpec((2, 2, SLAB_COLS, HEAD_DIM), lambda b, j, pt: (0, 0, 0, 0))]
        + _page_specs((PAGE_SIZE * SLABS_B, HEAD_DIM), pps),
        out_specs=[pl.BlockSpec((rows, B_KVW), lambda b, j, pt: (b * steps + j, 0)),
                   pl.BlockSpec((rows, B_KVW), lambda b, j, pt: (b * steps + j, 0)),
                   pl.BlockSpec((None, 8, B_KVW), lambda b, j, pt: (b, 0, 0))],
        scratch_shapes=[pltpu.VMEM((SLABS_B, rows, SLAB_COLS), F32)],
    )
    n_chunks = n_pages * CHUNKS_PER_PAGE
    return pl.pallas_call(
        kern,
        grid_spec=grid_spec,
        out_shape=[jax.ShapeDtypeStruct((nb * n_chunks, B_KVW), F32),
                   jax.ShapeDtypeStruct((nb * n_chunks, B_KVW), F32),
                   jax.ShapeDtypeStruct((nb, 8, B_KVW), F32)],
        compiler_params=_cparams(2),
        name="cmp_paged",
    )(page_table, x_new, pe, w,
      *([cache_cmp.reshape(cache_cmp.shape[0], PAGE_SIZE * SLABS_B, HEAD_DIM)] * pps))


QROWS_B = GB * 4
EXPAND_COLS = 1024


def _nsa_sample_kernel(pt_ref, q_ref, gate_ref, p0_ref, p1_ref, pn_ref, ov_ref, ksn_ref, vsn_ref,
                       win_ref, kwn_ref, vwn_ref, *rest, pps, past_len, n_cmp, n_sel):
    pages = rest[:pps]
    (o_ref, kc_ref, vc_ref, imp_ref, bias_ref, oc_ref, ow_ref, kbuf_ref, vbuf_ref,
     m_ref, l_ref, acc_ref) = rest[pps:]
    j = pl.program_id(1)
    nc = p0_ref.shape[0]
    tok = _row_token((QROWS_B, LANES))
    lane = lax.broadcasted_iota(I32, (QROWS_B, LANES), 1)
    rows_per_token = 2 * B_KV_HEADS

    @pl.when(j == 0)
    def _():
        rown = lax.broadcasted_iota(I32, (nc, B_KVW), 0)
        kc = p0_ref[...] + jnp.where(rown == nc - 1, jnp.broadcast_to(pn_ref[0:1, :], (nc, B_KVW)),
                                     pltpu.roll(p1_ref[...], nc - 1, 0))
        for g in range(B_KV_HEADS):
            kc_ref[g] = kc[:, (2 * g) * LANES:(2 * g + 1) * LANES].astype(BF16)
            vc_ref[g] = kc[:, (2 * g + 1) * LANES:(2 * g + 2) * LANES].astype(BF16)
        nidx = lax.broadcasted_iota(I32, (QROWS_B, nc), 1)
        tpos_c = past_len + _row_token((QROWS_B, nc))
        cvalid = jnp.where(nidx < n_cmp, nidx * CMP_STRIDE + (CMP_BLOCK - 1), 2 ** 30) <= tpos_c
        n_lt = imp_ref.shape[1] // LANES
        imp_shape = (QROWS_B, imp_ref.shape[1])
        jblk = lax.broadcasted_iota(I32, imp_shape, 1)
        tpos_i = past_len + _row_token(imp_shape)
        cur = tpos_i >> 6
        forced = (jblk == 0) | (jblk == cur) | (jblk == cur - 1)
        admissible = jblk * SLC_BLOCK <= tpos_i
        wb = win_ref.shape[0] // rows_per_token
        widx = lax.broadcasted_iota(I32, (QROWS_B, wb), 1)
        bias_w = jnp.concatenate([_mask_bias((widx - wb) > (_row_token((QROWS_B, wb)) - WINDOW)),
                                  _mask_bias(lane <= tok)], axis=1)
        for g in range(B_KV_HEADS):
            q = q_ref[g]
            s = lax.dot_general(q, kc_ref[g], _NT, preferred_element_type=F32)
            s = jnp.where(cvalid, s, NEG_BIG)
            e = jnp.where(cvalid, jnp.exp2(s - jnp.max(s, axis=1, keepdims=True)), 0.0)
            l = jnp.sum(e, axis=1, keepdims=True)
            p = e / jnp.where(l > 0.0, l, 1.0)
            oc_ref[g] = jnp.dot(p.astype(BF16), vc_ref[g], preferred_element_type=F32)
            pg = p[0:8] + p[8:16]
            pg = pg + pltpu.roll(pg, 4, 0)
            imp = _dot_f32(jnp.concatenate([pg, pg], axis=0), ov_ref[...])
            imp_ref[g * QROWS_B:(g + 1) * QROWS_B, :] = jnp.where(
                forced, jnp.inf, jnp.where(admissible, imp, -jnp.inf))

            kw = jnp.concatenate([win_ref[pl.ds(2 * g, wb, stride=rows_per_token), :].astype(BF16), kwn_ref[g]],
                                 axis=0)
            vw = jnp.concatenate([win_ref[pl.ds(2 * g + 1, wb, stride=rows_per_token), :].astype(BF16),
                                  vwn_ref[g]], axis=0)
            sw = lax.dot_general(q, kw, _NT, preferred_element_type=F32) + bias_w
            pw = jnp.exp2(sw - jnp.max(sw, axis=1, keepdims=True))
            ow_ref[g] = (jnp.dot(pw.astype(BF16), vw, preferred_element_type=F32)
                         / jnp.sum(pw, axis=1, keepdims=True))

        n_rows = B_KV_HEADS * QROWS_B
        thr, jpos, jneg = _topk_threshold(imp_ref, 1, n_lt, n_sel)
        lane_a = lax.broadcasted_iota(I32, (n_rows, LANES), 1)
        selb = jnp.concatenate(
            [jnp.where(_selected(imp_ref[:, c * LANES:(c + 1) * LANES], thr, jpos, jneg, lane_a + c * LANES),
                       1.0, 0.0) for c in range(n_lt)], axis=1).astype(BF16)
        total = bias_ref.shape[2]
        c0 = 0
        while c0 < total:
            w = min(EXPAND_COLS, total - c0)
            blk_of_key = (lax.broadcasted_iota(I32, (imp_ref.shape[1], w), 1) + c0) >> 6
            ex = jnp.where(blk_of_key == lax.broadcasted_iota(I32, (imp_ref.shape[1], w), 0), 1.0, 0.0)
            hit = jnp.dot(selb, ex.astype(BF16), preferred_element_type=F32)
            kpos = lax.broadcasted_iota(I32, (n_rows, w), 1) + c0
            visible = jnp.where(kpos <= past_len + _row_token((n_rows, w)), hit, 0.0) > 0.5
            bias = _mask_bias(visible)
            for g in range(B_KV_HEADS):
                bias_ref[g, :, c0:c0 + w] = bias[g * QROWS_B:(g + 1) * QROWS_B]
            c0 += w
        _flash_init(m_ref, l_ref, acc_ref)

    for k in range(pps):
        for g in range(B_KV_HEADS):
            rows = pl.ds(k * PAGE_SIZE, PAGE_SIZE)
            kbuf_ref[g, rows, :] = pages[k][pl.ds(2 * g, PAGE_SIZE, stride=rows_per_token), :].astype(BF16)
            vbuf_ref[g, rows, :] = pages[k][pl.ds(2 * g + 1, PAGE_SIZE, stride=rows_per_token), :].astype(BF16)
    for g in range(B_KV_HEADS):
        bias = [bias_ref[g, :, pl.ds(pl.multiple_of((j * pps + k) * PAGE_SIZE, PAGE_SIZE), PAGE_SIZE)]
                for k in range(pps)]
        _flash_step(q_ref[g], kbuf_ref[g], vbuf_ref[g], bias, m_ref, l_ref, acc_ref, g)

    @pl.when(j == pl.num_programs(1) - 1)
    def _():
        for g in range(B_KV_HEADS):
            _flash_step(q_ref[g], ksn_ref[g], vsn_ref[g], [bias_ref[g, :, past_len:past_len + LANES]],
                        m_ref, l_ref, acc_ref, g)
            o_ref[g] = (gate_ref[g, 0] * oc_ref[g] + gate_ref[g, 1] * (acc_ref[g] / l_ref[g])
                        + gate_ref[g, 2] * ow_ref[g])


def _nsa_sample(page_table, cache_slc, state_win, q_s, gates_s, p0, p1, pn, ks_new, vs_new, kw_new, vw_new,
                pps=16):
    nb, n_pages = page_table.shape
    pps = min(pps, n_pages)
    past_len = n_pages * PAGE_SIZE
    total = past_len + 4
    ncz = -(-total // CMP_STRIDE)
    nc = ncz - 1
    n_cmp = ncz - CMP_BLOCK // CMP_STRIDE + 1
    assert nc == n_pages * CHUNKS_PER_PAGE and n_cmp == nc
    n_slc = -(-total // SLC_BLOCK)
    imp_w = -(-n_slc // LANES) * LANES
    ov = _overlap_matrix(nc, n_cmp, imp_w)
    wb = state_win.shape[1]
    slabs = 2 * B_KV_HEADS
    per_b3 = lambda b, j, pt: (b, 0, 0)
    per_b4 = lambda b, j, pt: (b, 0, 0, 0)
    once = pl.Buffered(1)
    kern = functools.partial(_nsa_sample_kernel, pps=pps, past_len=past_len, n_cmp=n_cmp,
                             n_sel=min(SLC_TOPN, n_slc))
    new_spec = pl.BlockSpec((None, B_KV_HEADS, LANES, HEAD_DIM), per_b4)
    grid_spec = pltpu.PrefetchScalarGridSpec(
        num_scalar_prefetch=1,
        grid=(nb, n_pages // pps),
        in_specs=[pl.BlockSpec((None, B_KV_HEADS, QROWS_B, HEAD_DIM), per_b4),
                  pl.BlockSpec((None, B_KV_HEADS, 3, QROWS_B, HEAD_DIM), lambda b, j, pt: (b, 0, 0, 0, 0)),
                  pl.BlockSpec((nc, B_KVW), lambda b, j, pt: (b, 0)),
                  pl.BlockSpec((nc, B_KVW), lambda b, j, pt: (b, 0)),
                  pl.BlockSpec((None, 8, B_KVW), per_b3),
                  pl.BlockSpec((nc, imp_w), lambda b, j, pt: (0, 0), pipeline_mode=once),
                  new_spec, new_spec,
                  pl.BlockSpec((None, wb * slabs, HEAD_DIM), per_b3),
                  new_spec, new_spec]
        + _page_specs((PAGE_SIZE * slabs, HEAD_DIM), pps),
        out_specs=pl.BlockSpec((None, B_KV_HEADS, QROWS_B, HEAD_DIM), per_b4),
        scratch_shapes=[pltpu.VMEM((B_KV_HEADS, nc, HEAD_DIM), BF16),
                        pltpu.VMEM((B_KV_HEADS, nc, HEAD_DIM), BF16),
                        pltpu.VMEM((B_KV_HEADS * QROWS_B, imp_w), F32),
                        pltpu.VMEM((B_KV_HEADS, QROWS_B, past_len + LANES), F32),
                        pltpu.VMEM((B_KV_HEADS, QROWS_B, HEAD_DIM), F32),
                        pltpu.VMEM((B_KV_HEADS, QROWS_B, HEAD_DIM), F32),
                        pltpu.VMEM((B_KV_HEADS, pps * PAGE_SIZE, HEAD_DIM), BF16),
                        pltpu.VMEM((B_KV_HEADS, pps * PAGE_SIZE, HEAD_DIM), BF16),
                        pltpu.VMEM((B_KV_HEADS, QROWS_B, LANES), F32),
                        pltpu.VMEM((B_KV_HEADS, QROWS_B, LANES), F32),
                        pltpu.VMEM((B_KV_HEADS, QROWS_B, HEAD_DIM), F32)],
    )
    return pl.pallas_call(
        kern,
        grid_spec=grid_spec,
        out_shape=jax.ShapeDtypeStruct((nb, B_KV_HEADS, QROWS_B, HEAD_DIM), F32),
        compiler_params=_cparams(2),
        name="nsa_sample",
    )(page_table, q_s, gates_s, p0, p1, pn, ov, ks_new, vs_new,
      state_win.reshape(nb, wb * slabs, HEAD_DIM), kw_new, vw_new,
      *([cache_slc.reshape(cache_slc.shape[0], PAGE_SIZE * slabs, HEAD_DIM)] * pps))


def _nsa_sample_layer(x, cache_cmp, cache_slc, state_win, page_table, norm_g, w_in, q_g, k_g,
                      w_cmp, pe_cmp, w_o):
    nb, nt, _ = x.shape
    assert nt == 4
    n = nb * nt
    past_len = page_table.shape[1] * PAGE_SIZE
    pos = jnp.tile(past_len + jnp.arange(nt), nb)
    xf = x.reshape(n, D_MODEL)
    q, kvc, kvs, kvw, ks, vs, kw, vw, gates, z = _proj_b(
        xf, norm_g, _layout_w_b(w_in), q_g, k_g, _rope_tables(pos, ROT_DIM), n)
    wc, pe = _layout_cmp_slabs(w_cmp, pe_cmp)
    x_new = kvc.reshape(nb, nt, SLABS_B, HEAD_DIM).transpose(0, 2, 1, 3).reshape(nb, SLABS_B, 1, nt * HEAD_DIM)
    x_new = jnp.pad(x_new, ((0, 0), (0, 0), (0, 7), (0, SLAB_COLS - nt * HEAD_DIM)))
    p0, p1, pn = _cmp_paged(page_table, cache_cmp, x_new, wc, pe)
    q_s = q.reshape(nb, nt, B_KV_HEADS, GB, HEAD_DIM).transpose(0, 2, 3, 1, 4).reshape(nb, B_KV_HEADS, QROWS_B, HEAD_DIM)
    gt = gates[:, :3 * N_HEADS].reshape(nb, nt, B_KV_HEADS, GB, 3).transpose(0, 2, 4, 3, 1)
    gates_s = jnp.broadcast_to(gt.reshape(nb, B_KV_HEADS, 3, QROWS_B, 1), (nb, B_KV_HEADS, 3, QROWS_B, HEAD_DIM))
    new_keys = lambda a: _pad_keys(a.reshape(B_KV_HEADS, nb, nt, HEAD_DIM).transpose(1, 0, 2, 3), 2)
    o_s = _nsa_sample(page_table, cache_slc, state_win, q_s, gates_s, p0, p1, pn,
                      new_keys(ks), new_keys(vs), new_keys(kw), new_keys(vw))
    o = o_s.reshape(nb, B_KV_HEADS, GB, nt, HEAD_DIM).transpose(0, 3, 1, 2, 4).reshape(n, WIDTH)
    y = _out_proj(xf, o, z, w_o, n)
    shp = (nb, nt, B_KV_HEADS, 2, HEAD_DIM)
    win_out = jnp.concatenate([state_win, kvw.reshape(shp)], axis=1)[:, nt:]
    return y.reshape(nb, nt, D_MODEL), kvc.reshape(shp), kvs.reshape(shp), win_out


def kernel(x_prompt, x_sample, cache_a_kv, cache_a_idx, cache_b_cmp_kv, cache_b_slc_kv, state_b_win_kv,
           page_table, a_norm, a_w_in, a_q_norm, a_k_norm, a_w_o, b_norm, b_w_in, b_q_norm, b_k_norm,
           b_cmp_w, b_cmp_pe, b_w_o):
    yp, a_kv_p, a_idx_p = _dsa_prompt_layer(x_prompt, a_norm, a_w_in, a_q_norm, a_k_norm, a_w_o)
    yp, b_cmp_p, b_slc_p, b_win_p = _nsa_prompt_layer(yp, b_norm, b_w_in, b_q_norm, b_k_norm,
                                                      b_cmp_w, b_cmp_pe, b_w_o)
    ys, a_kv_s, a_idx_s = _dsa_sample_layer(x_sample, cache_a_kv, cache_a_idx, page_table,
                                            a_norm, a_w_in, a_q_norm, a_k_norm, a_w_o)
    ys, b_cmp_s, b_slc_s, b_win_s = _nsa_sample_layer(ys, cache_b_cmp_kv, cache_b_slc_kv, state_b_win_kv,
                                                      page_table, b_norm, b_w_in, b_q_norm, b_k_norm,
                                                      b_cmp_w, b_cmp_pe, b_w_o)
    return (yp, ys, a_kv_p, a_idx_p, a_kv_s, a_idx_s, b_cmp_p, b_slc_p, b_win_p, b_cmp_s, b_slc_s, b_win_s)
```

```python
import functools

import numpy as np
import jax
import jax.numpy as jnp
from jax import lax
from jax.experimental import pallas as pl
from jax.experimental.pallas import tpu as pltpu

F32 = jnp.float32
BF16 = jnp.bfloat16
I32 = jnp.int32

D_MODEL = 1024
N_HEADS = 8
HEAD_DIM = 128
WIDTH = N_HEADS * HEAD_DIM
ROT_DIM = HEAD_DIM // 4
ROPE_THETA = 500000.0
EPS = 1e-6
ATTN_SCALE = HEAD_DIM ** -0.5
Q_SCALE = ATTN_SCALE * 1.4426950408889634
PAGE_SIZE = 128
A_KV_HEADS = 4
IDX_HEADS = 8
IDX_DIM = 64
IDX_ROT = IDX_DIM // 4
TOPK_MAX = 256
B_KV_HEADS = 2
GB = N_HEADS // B_KV_HEADS
CMP_BLOCK = 32
CMP_STRIDE = 16
SLC_BLOCK = 64
SLC_TOPN = 16
WINDOW = 512

LANES = 128
NEG_BIG = -1e30
KEY_NEG_INF = -2139095041
VMEM_LIMIT = 56 * 1024 * 1024

_NT = (((1,), (1,)), ((), ()))


def _cparams(n_grid_dims):
    return pltpu.CompilerParams(
        dimension_semantics=("arbitrary",) * n_grid_dims,
        vmem_limit_bytes=VMEM_LIMIT)


def _rope_tables(pos, rot_dim):
    half = rot_dim // 2
    freq = ROPE_THETA ** (-jnp.arange(half, dtype=F32) / half)
    ang = pos.astype(F32)[:, None] * freq
    cos, sin = jnp.cos(ang), jnp.sin(ang)
    t = pos.shape[0]
    one = jnp.ones((t, LANES - rot_dim), F32)
    zero = jnp.zeros((t, LANES - rot_dim), F32)
    zh = jnp.zeros((t, half), F32)
    c = jnp.concatenate([cos, cos, one], axis=1)
    s1 = jnp.concatenate([-sin, zh, zero], axis=1)
    s2 = jnp.concatenate([zh, sin, zero], axis=1)
    return jnp.stack([c, s1, s2])


def _rope(t, tab_ref, half):
    return (t * tab_ref[0] + pltpu.roll(t, LANES - half, 1) * tab_ref[1]
            + pltpu.roll(t, half, 1) * tab_ref[2])


def _head_norm(t, g):
    return t * lax.rsqrt(jnp.mean(t * t, axis=-1, keepdims=True) + EPS) * g


A_COLS = 4 * WIDTH + LANES


def _layout_w_a(w_in):
    o = 0
    wq = w_in[:, o:o + WIDTH]; o += WIDTH
    wkv = w_in[:, o:o + 2 * A_KV_HEADS * HEAD_DIM]; o += 2 * A_KV_HEADS * HEAD_DIM
    wiq = w_in[:, o:o + IDX_HEADS * IDX_DIM]; o += IDX_HEADS * IDX_DIM
    wik = w_in[:, o:o + IDX_DIM]; o += IDX_DIM
    wiw = w_in[:, o:o + IDX_HEADS]; o += IDX_HEADS
    wz = w_in[:, o:o + WIDTH]
    d = w_in.shape[0]
    wiq = jnp.pad(wiq.reshape(d, IDX_HEADS, IDX_DIM), ((0, 0), (0, 0), (0, LANES - IDX_DIM))).reshape(d, -1)
    wikw = jnp.concatenate([wik, wiw, jnp.zeros((d, LANES - IDX_DIM - IDX_HEADS), w_in.dtype)], axis=1)
    return jnp.concatenate([wq, wkv, wiq, wikw, wz], axis=1).astype(BF16)


def _proj_a_kernel(x_ref, g_ref, w_ref, qg_ref, kg_ref, rt_ref, it_ref,
                   q_out, kv_out, k_out, v_out, iq_out, ikw_out, ikp_out, z_out, xn_ref):
    x = x_ref[...]
    ms = jnp.mean(x * x, axis=-1, keepdims=True)
    xn_ref[...] = (x * lax.rsqrt(ms + EPS) * g_ref[...]).astype(BF16)

    def seg(c0):
        return jnp.dot(xn_ref[...], w_ref[:, c0:c0 + LANES], preferred_element_type=F32)

    qg = qg_ref[...]
    kg = kg_ref[...]
    for h in range(N_HEADS):
        t = _rope(_head_norm(seg(h * LANES), qg), rt_ref, ROT_DIM // 2)
        q_out[:, h * LANES:(h + 1) * LANES] = (t * Q_SCALE).astype(BF16)
    base = WIDTH
    for g in range(A_KV_HEADS):
        k = _rope(_head_norm(seg(base + (2 * g) * LANES), kg), rt_ref, ROT_DIM // 2)
        v = seg(base + (2 * g + 1) * LANES)
        kv_out[:, (2 * g) * LANES:(2 * g + 1) * LANES] = k
        kv_out[:, (2 * g + 1) * LANES:(2 * g + 2) * LANES] = v
        k_out[g] = k.astype(BF16)
        v_out[g] = v.astype(BF16)
    base = 2 * WIDTH
    for h in range(IDX_HEADS):
        t = _rope(seg(base + h * LANES), it_ref, IDX_ROT // 2)
        iq_out[:, h * LANES:(h + 1) * LANES] = t.astype(BF16)
    base = 3 * WIDTH
    t = _rope(seg(base), it_ref, IDX_ROT // 2)
    lane = lax.broadcasted_iota(I32, t.shape, 1)
    w_scale = (IDX_HEADS * IDX_DIM) ** -0.5
    ikw_out[...] = jnp.where(lane < IDX_DIM, t, t * w_scale)
    ikp_out[...] = jnp.where(lane < IDX_DIM, t, 0.0).astype(BF16)
    base = 3 * WIDTH + LANES
    for h in range(N_HEADS):
        z_out[:, h * LANES:(h + 1) * LANES] = seg(base + h * LANES)


def _proj_a(x, norm_g, w_a, q_g, k_g, rope_tab, idx_tab, tm):
    n = x.shape[0]
    period = rope_tab.shape[1] // tm
    row = lambda i: (i, 0)
    const = lambda i: (0, 0)
    tab = lambda i: (0, i % period, 0)
    hm = lambda i: (0, i, 0)
    return pl.pallas_call(
        _proj_a_kernel,
        grid=(n // tm,),
        in_specs=[
            pl.BlockSpec((tm, D_MODEL), row),
            pl.BlockSpec((1, D_MODEL), const),
            pl.BlockSpec((D_MODEL, A_COLS), const),
            pl.BlockSpec((1, HEAD_DIM), const),
            pl.BlockSpec((1, HEAD_DIM), const),
            pl.BlockSpec((3, tm, LANES), tab),
            pl.BlockSpec((3, tm, LANES), tab),
        ],
        out_specs=[
            pl.BlockSpec((tm, WIDTH), row),
            pl.BlockSpec((tm, 2 * A_KV_HEADS * HEAD_DIM), row),
            pl.BlockSpec((A_KV_HEADS, tm, HEAD_DIM), hm),
            pl.BlockSpec((A_KV_HEADS, tm, HEAD_DIM), hm),
            pl.BlockSpec((tm, IDX_HEADS * LANES), row),
            pl.BlockSpec((tm, LANES), row),
            pl.BlockSpec((tm, LANES), row),
            pl.BlockSpec((tm, WIDTH), row),
        ],
        out_shape=[
            jax.ShapeDtypeStruct((n, WIDTH), BF16),
            jax.ShapeDtypeStruct((n, 2 * A_KV_HEADS * HEAD_DIM), F32),
            jax.ShapeDtypeStruct((A_KV_HEADS, n, HEAD_DIM), BF16),
            jax.ShapeDtypeStruct((A_KV_HEADS, n, HEAD_DIM), BF16),
            jax.ShapeDtypeStruct((n, IDX_HEADS * LANES), BF16),
            jax.ShapeDtypeStruct((n, LANES), F32),
            jax.ShapeDtypeStruct((n, LANES), BF16),
            jax.ShapeDtypeStruct((n, WIDTH), F32),
        ],
        scratch_shapes=[pltpu.VMEM((tm, D_MODEL), BF16)],
        compiler_params=_cparams(1),
        name="proj_a",
    )(x, norm_g.reshape(1, -1), w_a, q_g.reshape(1, -1), k_g.reshape(1, -1), rope_tab, idx_tab)


def _out_proj_kernel(x_ref, o_ref, z_ref, w_ref, y_ref):
    z = z_ref[...]
    h = (o_ref[...] * (z * jax.nn.sigmoid(z))).astype(BF16)
    y_ref[...] = x_ref[...] + jnp.dot(h, w_ref[...], preferred_element_type=F32)


def _out_proj(x, o, z, w_o, tm):
    n = x.shape[0]
    row = lambda i: (i, 0)
    return pl.pallas_call(
        _out_proj_kernel,
        grid=(n // tm,),
        in_specs=[pl.BlockSpec((tm, D_MODEL), row), pl.BlockSpec((tm, WIDTH), row),
                  pl.BlockSpec((tm, WIDTH), row), pl.BlockSpec((WIDTH, D_MODEL), lambda i: (0, 0))],
        out_specs=pl.BlockSpec((tm, D_MODEL), row),
        out_shape=jax.ShapeDtypeStruct((n, D_MODEL), F32),
        compiler_params=_cparams(1),
        name="out_proj",
    )(x, o, z, w_o.astype(BF16))


def _key_to_float(t):
    bits = t ^ ((t >> 31) & 0x7FFFFFFF)
    return pltpu.bitcast(bits, F32)


def _neg_zero(x):
    return (x == 0.0) & (pltpu.bitcast(x, I32) < 0)


WORD_BITS = 32


def _tie_words(sc_ref, n_groups, group, thr, r0):
    rc = thr.shape[0]
    n_words = -(-(sc_ref.shape[1] // LANES) // WORD_BITS)
    n_tiles = n_groups * group
    step = next(s for s in (16, 8, 4, 2, 1) if group % s == 0)
    pos, neg = [], []
    for w in range(n_words):
        trips = jnp.clip(n_tiles - w * WORD_BITS, 0, WORD_BITS) // step

        def body(q, acc, w=w):
            p, n = acc
            for u in range(step):
                b = q * step + u
                c0 = pl.multiple_of((w * WORD_BITS + b) * LANES, LANES)
                blk = sc_ref[r0:r0 + rc, pl.ds(c0, LANES)]
                bit = jnp.int32(1) << b
                tie = blk == thr
                nz = _neg_zero(blk)
                p = p | jnp.where(tie & ~nz, bit, 0)
                n = n | jnp.where(tie & nz, bit, 0)
            return p, n

        zero = jnp.zeros((rc, LANES), I32)
        p, n = lax.fori_loop(0, trips, body, (zero, zero))
        pos.append(p)
        neg.append(n)
    return pos, neg


def _lane_total(x):
    return jnp.broadcast_to(jnp.sum(x, axis=1, keepdims=True), x.shape)


def _tie_limit(words, quota, n_cols):
    rows = quota.shape[0]
    lane = lax.broadcasted_iota(I32, (rows, LANES), 1)

    def flagged_before(j):
        tile, lane_lim = j >> 7, j & (LANES - 1)
        total = jnp.zeros((rows, LANES), I32)
        for w, word in enumerate(words):
            m = jnp.clip(tile - w * WORD_BITS, 0, WORD_BITS)
            below = jnp.where(m >= WORD_BITS, -1, (jnp.int32(1) << jnp.minimum(m, WORD_BITS - 1)) - 1)
            total = total + lax.population_count(word & below)
            here = ((tile >> 5) == w) & (lane < lane_lim)
            total = total + jnp.where(here, (word >> (tile & (WORD_BITS - 1))) & 1, 0)
        return _lane_total(total)

    j = jnp.zeros((rows, LANES), I32)
    for bit in reversed(range(max(1, int(n_cols).bit_length()))):
        cand = j + (1 << bit)
        j = jnp.where(flagged_before(cand) <= quota, cand, j)
    return j


def _row_count(sc_ref, n_groups, group, row_chunk, pred, args):
    rows = sc_ref.shape[0]
    rc = min(row_chunk, rows)
    starts = list(range(0, rows, rc))
    parts = []
    for i in range(0, len(starts), 2):
        pair = starts[i:i + 2]
        a = [[x[r0:r0 + rc] if hasattr(x, "shape") and x.shape else x for x in args] for r0 in pair]

        def body(j, cs, pair=pair, a=a):
            cs = list(cs)
            for u in range(group):
                c0 = pl.multiple_of((j * group + u) * LANES, LANES)
                for n, r0 in enumerate(pair):
                    cs[n] = cs[n] + jnp.where(pred(sc_ref[r0:r0 + rc, pl.ds(c0, LANES)], c0, *a[n]), 1, 0)
            return tuple(cs)

        parts.extend(lax.fori_loop(0, n_groups, body, tuple(jnp.zeros((rc, LANES), I32) for _ in pair)))
    c = parts[0] if len(parts) == 1 else jnp.concatenate(parts, axis=0)
    return jnp.broadcast_to(jnp.sum(c, axis=1, keepdims=True), (rows, LANES))


def _topk_threshold(sc_ref, n_groups, group, k, row_chunk=64):
    rows = sc_ref.shape[0]
    count = functools.partial(_row_count, sc_ref, n_groups, group, row_chunk)

    def count_ge(t):
        return count(lambda blk, c0, thr: blk >= thr, [_key_to_float(t)])

    few_positive = count(lambda blk, c0: blk > 0.0, []) < k
    state = (jnp.int32(0), jnp.full((rows, LANES), -2 ** 31, I32), jnp.full((rows, LANES), 2 ** 30, I32))

    def unsettled(st):
        it, t, cnt = st
        open_rows = jnp.where(few_positive & (t == 0), 0, jnp.where(cnt != k, 1, 0))
        return jnp.logical_and(it < 32, jnp.max(open_rows) > 0)

    def bit_pair(st):
        it, t, cnt = st
        for step in range(2):
            cand = t + (jnp.int32(1) << (31 - it - step))
            c = count_ge(cand)
            ok = c >= k
            t = jnp.where(ok, cand, t)
            cnt = jnp.where(ok, c, cnt)
        return it + 2, t, cnt

    _, t, n_ge = lax.while_loop(unsettled, bit_pair, state)
    thr = _key_to_float(jnp.maximum(t, KEY_NEG_INF))
    short = thr == -jnp.inf
    excess = jnp.where(short, 0, n_ge - k)

    rc = min(row_chunk, rows)
    n_cols = sc_ref.shape[1]

    def chunk_limits(r0):
        rows_c = slice(r0, r0 + rc)

        def run():
            pos, neg = _tie_words(sc_ref, n_groups, group, thr[rows_c], r0)
            n_pos = _lane_total(sum(lax.population_count(w) for w in pos))
            n_neg = _lane_total(sum(lax.population_count(w) for w in neg))
            need = k - (n_ge[rows_c] - (n_pos + n_neg))
            need_pos = jnp.minimum(need, n_pos)
            need_neg = need - need_pos
            jn = lax.cond(jnp.max(need_neg) > 0, lambda: _tie_limit(neg, need_neg, n_cols),
                          lambda: jnp.zeros((rc, LANES), I32))
            return _tie_limit(pos, need_pos, n_cols), jn

        unbounded = lambda: (jnp.full((rc, LANES), 2 ** 30, I32), jnp.full((rc, LANES), 2 ** 30, I32))
        return lax.cond(jnp.max(excess[rows_c]) > 0, run, unbounded)

    limits = [chunk_limits(r0) for r0 in range(0, rows, rc)]
    cat = lambda parts: parts[0] if len(parts) == 1 else jnp.concatenate(parts, axis=0)
    jpos, jneg = cat([p for p, _ in limits]), cat([n for _, n in limits])
    return thr, jnp.where(short, 0, jpos), jnp.where(short, 0, jneg)


def _selected(s, thr, jpos, jneg, col):
    return (s > thr) | ((s == thr) & (col < jnp.where(_neg_zero(s), jneg, jpos)))


def _dsa_prompt_kernel(iq_ref, ikw_ref, ik_ref, q_ref, k_ref, v_ref, o_ref,
                       sc_ref, m_ref, l_ref, acc_ref, *, tq, tk, n_keep):
    i = pl.program_id(1)
    q0 = i * tq
    nkb = lax.div(q0 + tq + tk - 1, tk)
    tiles_per_kb = tk // LANES

    def score_body(j, carry):
        k0 = pl.multiple_of(j * tk, tk)
        ikb = ik_ref[pl.ds(k0, tk), :]
        acc = jnp.zeros((tq, tk), F32)
        for h in range(IDX_HEADS):
            lg = lax.dot_general(iq_ref[:, h * LANES:(h + 1) * LANES], ikb, _NT,
                                 preferred_element_type=F32)
            w = ikw_ref[:, IDX_DIM + h:IDX_DIM + h + 1]
            acc = acc + w * jnp.maximum(lg, 0.0)
        kpos = k0 + lax.broadcasted_iota(I32, (tq, tk), 1)
        tpos = q0 + lax.broadcasted_iota(I32, (tq, tk), 0)
        sc_ref[:, pl.ds(k0, tk)] = jnp.where(kpos <= tpos, acc, -jnp.inf)
        return carry

    lax.fori_loop(0, nkb, score_body, 0)

    thr, jpos, jneg = _topk_threshold(sc_ref, nkb, tiles_per_kb, n_keep)

    half = tq // 2
    lane = lax.broadcasted_iota(I32, (half, LANES), 1)

    def bias_body(j, carry):
        c0 = pl.multiple_of(j * LANES, LANES)
        for r0 in (0, half):
            sel = _selected(sc_ref[r0:r0 + half, pl.ds(c0, LANES)], thr[r0:r0 + half], jpos[r0:r0 + half],
                            jneg[r0:r0 + half], lane + c0)
            sc_ref[r0:r0 + half, pl.ds(c0, LANES)] = _mask_bias(sel)
        return carry

    lax.fori_loop(0, nkb * tiles_per_kb, bias_body, 0)

    _flash_init(m_ref, l_ref, acc_ref)

    def attn_body(j, carry):
        k0 = pl.multiple_of(j * tk, tk)
        bias = [sc_ref[:, pl.ds(k0 + t * LANES, LANES)] for t in range(tiles_per_kb)]
        for h in range(N_HEADS):
            g = h // (N_HEADS // A_KV_HEADS)
            _flash_step(q_ref[:, h * LANES:(h + 1) * LANES], k_ref[g, pl.ds(k0, tk), :],
                        v_ref[g, pl.ds(k0, tk), :], bias, m_ref, l_ref, acc_ref, h)
        return carry

    lax.fori_loop(0, nkb, attn_body, 0)
    for h in range(N_HEADS):
        o_ref[:, h * LANES:(h + 1) * LANES] = acc_ref[h] / l_ref[h]


def _dsa_prompt(iq, ikw, ikp, q, k_hm, v_hm, batch, seq, n_keep, tq=512, tk=512):
    tq = min(tq, seq)
    tk = min(tk, seq)
    nq = seq // tq
    n = batch * seq
    row = lambda b, i: (b * nq + i, 0)
    kern = functools.partial(_dsa_prompt_kernel, tq=tq, tk=tk, n_keep=n_keep)
    once = pl.Buffered(1)
    return pl.pallas_call(
        kern,
        grid=(batch, nq),
        in_specs=[
            pl.BlockSpec((tq, IDX_HEADS * LANES), row, pipeline_mode=once),
            pl.BlockSpec((tq, LANES), row),
            pl.BlockSpec((seq, LANES), lambda b, i: (b, 0), pipeline_mode=once),
            pl.BlockSpec((tq, WIDTH), row, pipeline_mode=once),
            pl.BlockSpec((A_KV_HEADS, seq, HEAD_DIM), lambda b, i: (0, b, 0), pipeline_mode=once),
            pl.BlockSpec((A_KV_HEADS, seq, HEAD_DIM), lambda b, i: (0, b, 0), pipeline_mode=once),
        ],
        out_specs=pl.BlockSpec((tq, WIDTH), row),
        out_shape=jax.ShapeDtypeStruct((n, WIDTH), F32),
        scratch_shapes=[
            pltpu.VMEM((tq, seq + LANES), F32),
            pltpu.VMEM((N_HEADS, tq, LANES), F32),
            pltpu.VMEM((N_HEADS, tq, LANES), F32),
            pltpu.VMEM((N_HEADS, tq, HEAD_DIM), F32),
        ],
        compiler_params=_cparams(2),
        name="dsa_prompt",
    )(iq, ikw, ikp, q, k_hm, v_hm)


def _dsa_prompt_layer(x, norm_g, w_in, q_g, k_g, w_o):
    b, s, _ = x.shape
    n = b * s
    pos = jnp.arange(s)
    tm = min(256, s)
    xf = x.reshape(n, D_MODEL)
    q, kv, k_hm, v_hm, iq, ikw, ikp, z = _proj_a(
        xf, norm_g, _layout_w_a(w_in), q_g, k_g, _rope_tables(pos, ROT_DIM), _rope_tables(pos, IDX_ROT), tm)
    o = _dsa_prompt(iq, ikw, ikp, q, k_hm, v_hm, b, s, min(TOPK_MAX, s // 4))
    y = _out_proj(xf, o, z, w_o, tm)
    return (y.reshape(b, s, D_MODEL), kv.reshape(b, s, A_KV_HEADS, 2, HEAD_DIM),
            ikw[:, :IDX_DIM].reshape(b, s, IDX_DIM))


B_KVW = 2 * B_KV_HEADS * HEAD_DIM
B_COLS = 2 * WIDTH + 3 * B_KVW + LANES


def _layout_w_b(w_in):
    o = WIDTH + 3 * B_KVW
    d = w_in.shape[0]
    wg = jnp.concatenate([w_in[:, o:o + 3 * N_HEADS], jnp.zeros((d, LANES - 3 * N_HEADS), w_in.dtype)], axis=1)
    return jnp.concatenate([w_in[:, :o], wg, w_in[:, o + 3 * N_HEADS:]], axis=1).astype(BF16)


def _proj_b_kernel(x_ref, g_ref, w_ref, qg_ref, kg_ref, rt_ref,
                   q_out, kvc_out, kvs_out, kvw_out, ks_out, vs_out, kw_out, vw_out, gate_out, z_out, xn_ref):
    x = x_ref[...]
    ms = jnp.mean(x * x, axis=-1, keepdims=True)
    xn_ref[...] = (x * lax.rsqrt(ms + EPS) * g_ref[...]).astype(BF16)

    def seg(c0):
        return jnp.dot(xn_ref[...], w_ref[:, c0:c0 + LANES], preferred_element_type=F32)

    qg = qg_ref[...]
    for h in range(N_HEADS):
        t = _rope(_head_norm(seg(h * LANES), qg), rt_ref, ROT_DIM // 2)
        q_out[:, h * LANES:(h + 1) * LANES] = (t * Q_SCALE).astype(BF16)
    branch_out = ((kvc_out, None, None), (kvs_out, ks_out, vs_out), (kvw_out, kw_out, vw_out))
    for br, (kv_out, k_out, v_out) in enumerate(branch_out):
        base = WIDTH + br * B_KVW
        kg = kg_ref[br:br + 1, :]
        for g in range(B_KV_HEADS):
            k = _rope(_head_norm(seg(base + (2 * g) * LANES), kg), rt_ref, ROT_DIM // 2)
            v = seg(base + (2 * g + 1) * LANES)
            kv_out[:, (2 * g) * LANES:(2 * g + 1) * LANES] = k
            kv_out[:, (2 * g + 1) * LANES:(2 * g + 2) * LANES] = v
            if k_out is not None:
                k_out[g] = k.astype(BF16)
                v_out[g] = v.astype(BF16)
    base = WIDTH + 3 * B_KVW
    gate_out[...] = jax.nn.sigmoid(seg(base))
    base += LANES
    for h in range(N_HEADS):
        z_out[:, h * LANES:(h + 1) * LANES] = seg(base + h * LANES)


def _proj_b(x, norm_g, w_b, q_g, k_g, rope_tab, tm):
    n = x.shape[0]
    period = rope_tab.shape[1] // tm
    row = lambda i: (i, 0)
    const = lambda i: (0, 0)
    hm = lambda i: (0, i, 0)
    kv_spec = pl.BlockSpec((tm, B_KVW), row)
    hm_spec = pl.BlockSpec((B_KV_HEADS, tm, HEAD_DIM), hm)
    kv_shape = jax.ShapeDtypeStruct((n, B_KVW), F32)
    hm_shape = jax.ShapeDtypeStruct((B_KV_HEADS, n, HEAD_DIM), BF16)
    return pl.pallas_call(
        _proj_b_kernel,
        grid=(n // tm,),
        in_specs=[
            pl.BlockSpec((tm, D_MODEL), row),
            pl.BlockSpec((1, D_MODEL), const),
            pl.BlockSpec((D_MODEL, B_COLS), const),
            pl.BlockSpec((1, HEAD_DIM), const),
            pl.BlockSpec((3, HEAD_DIM), const),
            pl.BlockSpec((3, tm, LANES), lambda i: (0, i % period, 0)),
        ],
        out_specs=[pl.BlockSpec((tm, WIDTH), row), kv_spec, kv_spec, kv_spec,
                   hm_spec, hm_spec, hm_spec, hm_spec,
                   pl.BlockSpec((tm, LANES), row), pl.BlockSpec((tm, WIDTH), row)],
        out_shape=[jax.ShapeDtypeStruct((n, WIDTH), BF16), kv_shape, kv_shape, kv_shape,
                   hm_shape, hm_shape, hm_shape, hm_shape,
                   jax.ShapeDtypeStruct((n, LANES), F32), jax.ShapeDtypeStruct((n, WIDTH), F32)],
        scratch_shapes=[pltpu.VMEM((tm, D_MODEL), BF16)],
        compiler_params=_cparams(1),
        name="proj_b",
    )(x, norm_g.reshape(1, -1), w_b, q_g.reshape(1, -1), k_g, rope_tab)


CHUNK_COLS = CMP_STRIDE * B_KVW


def _layout_cmp(w_cmp, pe_cmp):
    r = CMP_BLOCK // CMP_STRIDE
    eye_g = jnp.eye(B_KV_HEADS, dtype=w_cmp.dtype)
    eye_c = jnp.eye(2, dtype=w_cmp.dtype)
    wj = w_cmp.reshape(r, CMP_STRIDE, 2, HEAD_DIM, HEAD_DIM)
    w = jnp.einsum('jlcde,gh,ck->jlgcdhke', wj, eye_g, eye_c).reshape(r, CHUNK_COLS, B_KVW)
    pe = jnp.broadcast_to(pe_cmp.reshape(r, CMP_STRIDE, 1, 2, HEAD_DIM),
                          (r, CMP_STRIDE, B_KV_HEADS, 2, HEAD_DIM)).reshape(r, 1, CHUNK_COLS)
    return w.astype(BF16), pe


def _cmp_mm_kernel(x_ref, pe_ref, w_ref, p0_ref, p1_ref):
    x = x_ref[...]
    p0_ref[...] = jnp.dot((x + pe_ref[0]).astype(BF16), w_ref[0], preferred_element_type=F32)
    p1_ref[...] = jnp.dot((x + pe_ref[1]).astype(BF16), w_ref[1], preferred_element_type=F32)


def _cmp_mm(x, w, pe, tr):
    r = x.shape[0]
    row = lambda i: (i, 0)
    once = pl.Buffered(1)
    return pl.pallas_call(
        _cmp_mm_kernel,
        grid=(r // tr,),
        in_specs=[pl.BlockSpec((tr, CHUNK_COLS), row),
                  pl.BlockSpec((2, 1, CHUNK_COLS), lambda i: (0, 0, 0)),
                  pl.BlockSpec((2, CHUNK_COLS, B_KVW), lambda i: (0, 0, 0), pipeline_mode=once)],
        out_specs=[pl.BlockSpec((tr, B_KVW), row), pl.BlockSpec((tr, B_KVW), row)],
        out_shape=[jax.ShapeDtypeStruct((r, B_KVW), F32), jax.ShapeDtypeStruct((r, B_KVW), F32)],
        compiler_params=_cparams(1),
        name="cmp_mm",
    )(x, pe, w)


def _overlap_matrix(n_rows, n_cmp, n_cols):
    i = np.arange(n_rows)[:, None]
    j = np.arange(n_cols)[None, :]
    lo = np.maximum(i * CMP_STRIDE, j * SLC_BLOCK)
    hi = np.minimum(i * CMP_STRIDE + CMP_BLOCK, (j + 1) * SLC_BLOCK)
    ov = np.maximum(hi - lo, 0) / CMP_STRIDE
    ov = np.where(i < n_cmp, ov, 0.0)
    return jnp.asarray(ov, dtype=BF16)


def _dot_f32(a, b_bf16):
    hi = a.astype(BF16)
    r1 = a - hi.astype(F32)
    mid = r1.astype(BF16)
    lo = (r1 - mid.astype(F32)).astype(BF16)
    d = lambda p: jnp.dot(p, b_bf16, preferred_element_type=F32)
    return d(hi) + d(mid) + d(lo)


def _mask_bias(valid):
    return jnp.where(valid, 0.0, NEG_BIG)


def _flash_step(q, k, v, bias_tiles, m_ref, l_ref, acc_ref, h):
    s = lax.dot_general(q, k, _NT, preferred_element_type=F32)
    st = [s[:, t * LANES:(t + 1) * LANES] + b for t, b in enumerate(bias_tiles)]
    rows = s.shape[0]
    mt = functools.reduce(jnp.maximum, st)
    m_prev = m_ref[h]
    m_new = jnp.maximum(m_prev, jnp.broadcast_to(jnp.max(mt, axis=1, keepdims=True), (rows, LANES)))
    alpha = jnp.exp2(m_prev - m_new)
    pt = [jnp.exp2(x - m_new) for x in st]
    lt = functools.reduce(jnp.add, pt)
    l_ref[h] = alpha * l_ref[h] + jnp.broadcast_to(jnp.sum(lt, axis=1, keepdims=True), (rows, LANES))
    p = pt[0] if len(pt) == 1 else jnp.concatenate(pt, axis=1)
    acc_ref[h] = alpha * acc_ref[h] + jnp.dot(p.astype(BF16), v, preferred_element_type=F32)
    m_ref[h] = m_new


def _flash_init(m_ref, l_ref, acc_ref):
    m_ref[...] = jnp.full(m_ref.shape, NEG_BIG, F32)
    l_ref[...] = jnp.zeros(l_ref.shape, F32)
    acc_ref[...] = jnp.zeros(acc_ref.shape, F32)


def _topk_cols(xt_ref, sel_ref, k):
    nb, c = xt_ref.shape
    slabs = [(r0, xt_ref[r0:r0 + 8, :]) for r0 in range(0, nb, 8)]
    row = lax.broadcasted_iota(I32, (8, c), 0)

    def count(pred):
        acc = jnp.zeros((8, c), I32)
        for r0, blk in slabs:
            acc = acc + jnp.where(pred(blk, r0), 1, 0)
        return jnp.broadcast_to(jnp.sum(acc, axis=0, keepdims=True), (8, c))

    def bit_body(it, t):
        cand = t + (jnp.int32(1) << (31 - it))
        cand_f = _key_to_float(cand)
        return jnp.where(count(lambda blk, r0: blk >= cand_f) >= k, cand, t)

    zero = jnp.zeros((8, c), I32)
    t = lax.fori_loop(0, 32, bit_body, jnp.full((8, c), -2 ** 31, I32))
    thr = _key_to_float(jnp.maximum(t, KEY_NEG_INF))
    short = thr == -jnp.inf
    n_gt = count(lambda blk, r0: blk > thr)
    excess = jnp.where(short, 0, count(lambda blk, r0: blk >= thr) - k)
    need = k - n_gt
    n_bits = max(1, int(nb).bit_length())

    def tie_limit():
        jl = zero
        for it in range(n_bits):
            cand = jl + (1 << (n_bits - 1 - it))
            cnt = count(lambda blk, r0: (blk == thr) & ((row + r0) < cand))
            jl = jnp.where(cnt <= need, cand, jl)
        return jl

    jlim = lax.cond(jnp.max(excess) > 0, tie_limit, lambda: jnp.full((8, c), 2 ** 30, I32))
    jlim = jnp.where(short, 0, jlim)
    for r0, blk in slabs:
        sel_ref[r0:r0 + 8, :] = jnp.where((blk > thr) | ((blk == thr) & ((row + r0) < jlim)), 1.0, 0.0)


def _nsa_prompt_kernel(q_ref, gate_ref, p0_ref, p1_ref, ov_ref, ks_ref, vs_ref, kw_ref, vw_ref, o_ref,
                       kc_ref, vc_ref, impt_ref, selt_ref, oc_ref, os_ref, m_ref, l_ref, acc_ref,
                       *, tq, tk, wk, n_cmp, n_sel):
    i = pl.program_id(1)
    q0 = i * tq
    nc = p0_ref.shape[0]

    @pl.when(i == 0)
    def _():
        kc = p0_ref[...] + pltpu.roll(p1_ref[...], nc - 1, 0)
        for g in range(B_KV_HEADS):
            kc_ref[g] = kc[:, (2 * g) * LANES:(2 * g + 1) * LANES].astype(BF16)
            vc_ref[g] = kc[:, (2 * g + 1) * LANES:(2 * g + 2) * LANES].astype(BF16)

    tpos_c = q0 + lax.broadcasted_iota(I32, (tq, nc), 0)
    nidx = lax.broadcasted_iota(I32, (tq, nc), 1)
    cvalid = jnp.where(nidx < n_cmp, nidx * CMP_STRIDE + (CMP_BLOCK - 1), 2 ** 30) <= tpos_c
    tpos = q0 + lax.broadcasted_iota(I32, (tq, LANES), 0)
    jblk = lax.broadcasted_iota(I32, (tq, LANES), 1)
    cur = tpos >> 6
    forced = (jblk == 0) | (jblk == cur) | (jblk == cur - 1)
    admissible = jblk * SLC_BLOCK <= tpos
    sel = []
    for g in range(B_KV_HEADS):
        pg = jnp.zeros((tq, nc), F32)
        for r in range(GB):
            h = g * GB + r
            s = lax.dot_general(q_ref[:, h * LANES:(h + 1) * LANES], kc_ref[g], _NT,
                                preferred_element_type=F32)
            s = jnp.where(cvalid, s, NEG_BIG)
            e = jnp.where(cvalid, jnp.exp2(s - jnp.max(s, axis=1, keepdims=True)), 0.0)
            l = jnp.sum(e, axis=1, keepdims=True)
            p = e / jnp.where(l > 0.0, l, 1.0)
            oc_ref[h] = jnp.dot(p.astype(BF16), vc_ref[g], preferred_element_type=F32)
            pg = pg + p
        imp = _dot_f32(pg, ov_ref[...])
        imp = jnp.where(forced, jnp.inf, jnp.where(admissible, imp, -jnp.inf))
        impt_ref[...] = imp.T
        _topk_cols(impt_ref, selt_ref, n_sel)
        sel.append(selt_ref[...].T.astype(BF16))

    nkb = lax.div(q0 + tq + tk - 1, tk)
    _flash_init(m_ref, l_ref, acc_ref)
    blk_row = lax.broadcasted_iota(I32, (LANES, tk), 0)
    key_col = lax.broadcasted_iota(I32, (LANES, tk), 1)
    tpos_k = q0 + lax.broadcasted_iota(I32, (tq, tk), 0)
    kcol = lax.broadcasted_iota(I32, (tq, tk), 1)

    def slc_body(j, carry):
        k0 = pl.multiple_of(j * tk, tk)
        expand = jnp.where(((key_col + k0) >> 6) == blk_row, 1.0, 0.0).astype(BF16)
        causal = (kcol + k0) <= tpos_k
        for g in range(B_KV_HEADS):
            hit = jnp.dot(sel[g], expand, preferred_element_type=F32)
            bias = _mask_bias(jnp.where(causal, hit, 0.0) > 0.5)
            bias = [bias[:, t * LANES:(t + 1) * LANES] for t in range(tk // LANES)]
            for r in range(GB):
                h = g * GB + r
                _flash_step(q_ref[:, h * LANES:(h + 1) * LANES], ks_ref[g, pl.ds(k0, tk), :],
                            vs_ref[g, pl.ds(k0, tk), :], bias, m_ref, l_ref, acc_ref, h)
        return carry

    lax.fori_loop(0, nkb, slc_body, 0)
    for h in range(N_HEADS):
        os_ref[h] = acc_ref[h] / l_ref[h]

    w0 = pl.multiple_of(jnp.maximum(q0 - WINDOW, 0), tq)
    kpos = w0 + lax.broadcasted_iota(I32, (tq, wk), 1)
    tpos_w = q0 + lax.broadcasted_iota(I32, (tq, wk), 0)
    bias_w = _mask_bias(jnp.where(kpos <= tpos_w, kpos, -2 ** 30) > tpos_w - WINDOW)
    for h in range(N_HEADS):
        g = h // GB
        s = lax.dot_general(q_ref[:, h * LANES:(h + 1) * LANES], kw_ref[g, pl.ds(w0, wk), :], _NT,
                            preferred_element_type=F32) + bias_w
        p = jnp.exp2(s - jnp.max(s, axis=1, keepdims=True))
        ow = (jnp.dot(p.astype(BF16), vw_ref[g, pl.ds(w0, wk), :], preferred_element_type=F32)
              / jnp.sum(p, axis=1, keepdims=True))
        o_ref[:, h * LANES:(h + 1) * LANES] = (gate_ref[:, 3 * h:3 * h + 1] * oc_ref[h]
                                               + gate_ref[:, 3 * h + 1:3 * h + 2] * os_ref[h]
                                               + gate_ref[:, 3 * h + 2:3 * h + 3] * ow)


def _nsa_prompt(q, gates, p0, p1, ks, vs, kw, vw, batch, seq, tq=512, tk=512):
    tq = min(tq, seq)
    tk = min(tk, seq)
    nq = seq // tq
    n = batch * seq
    nc = seq // CMP_STRIDE
    n_cmp = nc - CMP_BLOCK // CMP_STRIDE + 1
    n_slc = seq // SLC_BLOCK
    assert n_slc <= LANES and WINDOW % tq == 0
    ov = _overlap_matrix(nc, n_cmp, LANES)
    row = lambda b, i: (b * nq + i, 0)
    per_b = lambda b, i: (b, 0)
    hm_b = lambda b, i: (0, b, 0)
    once = pl.Buffered(1)
    kern = functools.partial(_nsa_prompt_kernel, tq=tq, tk=tk, wk=min(WINDOW + tq, seq), n_cmp=n_cmp,
                             n_sel=min(SLC_TOPN, n_slc))
    hm_spec = pl.BlockSpec((B_KV_HEADS, seq, HEAD_DIM), hm_b, pipeline_mode=once)
    return pl.pallas_call(
        kern,
        grid=(batch, nq),
        in_specs=[
            pl.BlockSpec((tq, WIDTH), row),
            pl.BlockSpec((tq, LANES), row),
            pl.BlockSpec((nc, B_KVW), per_b, pipeline_mode=once),
            pl.BlockSpec((nc, B_KVW), per_b, pipeline_mode=once),
            pl.BlockSpec((nc, LANES), lambda b, i: (0, 0), pipeline_mode=once),
            hm_spec, hm_spec, hm_spec, hm_spec,
        ],
        out_specs=pl.BlockSpec((tq, WIDTH), row),
        out_shape=jax.ShapeDtypeStruct((n, WIDTH), F32),
        scratch_shapes=[
            pltpu.VMEM((B_KV_HEADS, nc, HEAD_DIM), BF16),
            pltpu.VMEM((B_KV_HEADS, nc, HEAD_DIM), BF16),
            pltpu.VMEM((LANES, tq), F32),
            pltpu.VMEM((LANES, tq), F32),
            pltpu.VMEM((N_HEADS, tq, HEAD_DIM), F32),
            pltpu.VMEM((N_HEADS, tq, HEAD_DIM), F32),
            pltpu.VMEM((N_HEADS, tq, LANES), F32),
            pltpu.VMEM((N_HEADS, tq, LANES), F32),
            pltpu.VMEM((N_HEADS, tq, HEAD_DIM), F32),
        ],
        compiler_params=_cparams(2),
        name="nsa_prompt",
    )(q, gates, p0, p1, ov, ks, vs, kw, vw)


def _nsa_prompt_layer(x, norm_g, w_in, q_g, k_g, w_cmp, pe_cmp, w_o):
    b, s, _ = x.shape
    n = b * s
    tm = min(256, s)
    xf = x.reshape(n, D_MODEL)
    q, kvc, kvs, kvw, ks, vs, kw, vw, gates, z = _proj_b(
        xf, norm_g, _layout_w_b(w_in), q_g, k_g, _rope_tables(jnp.arange(s), ROT_DIM), tm)
    wc, pe = _layout_cmp(w_cmp, pe_cmp)
    nc = s // CMP_STRIDE
    p0, p1 = _cmp_mm(kvc.reshape(b * nc, CHUNK_COLS), wc, pe, min(128, nc))
    o = _nsa_prompt(q, gates, p0, p1, ks, vs, kw, vw, b, s)
    y = _out_proj(xf, o, z, w_o, tm)
    shp = (b, s, B_KV_HEADS, 2, HEAD_DIM)
    wlen = min(WINDOW, s)
    return (y.reshape(b, s, D_MODEL), kvc.reshape(shp), kvs.reshape(shp),
            kvw.reshape(shp)[:, s - wlen:])


def _page_specs(block, pages_per_step):
    return [pl.BlockSpec((None,) + block,
                         lambda b, j, pt, k=k: (pt[b, j * pages_per_step + k],) + (0,) * len(block))
            for k in range(pages_per_step)]


def _split_tiles(n_tiles, max_group=16):
    group = max(g for g in range(1, max_group + 1) if n_tiles % g == 0)
    return n_tiles // group, group


def _row_token(shape):
    return lax.broadcasted_iota(I32, shape, 0) & 3


def _dsa_sample_score_kernel(pt_ref, iq_ref, iw_ref, ikn_ref, *rest, pps, past_len, n_keep):
    pages = rest[:pps]
    sc_ref, thr_ref, jlim_ref = rest[pps:]
    j = pl.program_id(1)
    iq = iq_ref[:, :IDX_DIM]

    def score(keys_bf16):
        lg = lax.dot_general(iq, keys_bf16, _NT, preferred_element_type=F32)
        acc = jnp.zeros((8, LANES), F32)
        for h in range(IDX_HEADS):
            acc = acc + iw_ref[h * 8:(h + 1) * 8, :] * jnp.maximum(lg[h * 8:(h + 1) * 8, :], 0.0)
        return acc

    for k in range(pps):
        c0 = pl.multiple_of((j * pps + k) * PAGE_SIZE, PAGE_SIZE)
        sc_ref[:, pl.ds(c0, PAGE_SIZE)] = score(pages[k][...].astype(BF16))

    @pl.when(j == pl.num_programs(1) - 1)
    def _():
        s_new = score(ikn_ref[:, :IDX_DIM])
        col = lax.broadcasted_iota(I32, (8, LANES), 1)
        sc_ref[:, past_len:past_len + LANES] = jnp.where(col <= _row_token((8, LANES)), s_new, -jnp.inf)
        n_groups, group = _split_tiles(sc_ref.shape[1] // LANES)
        thr, jpos, jneg = _topk_threshold(sc_ref, n_groups, group, n_keep)
        thr_ref[...] = thr
        jlim_ref[0] = jpos
        jlim_ref[1] = jneg


def _dsa_sample_scores(page_table, cache_idx, iq_s, iw_s, ik_new, n_keep, pps=8):
    nb, n_pages = page_table.shape
    past_len = n_pages * PAGE_SIZE
    scw = past_len + LANES
    per_b = lambda b, j, pt: (b, 0, 0)
    kern = functools.partial(_dsa_sample_score_kernel, pps=pps, past_len=past_len, n_keep=n_keep)
    grid_spec = pltpu.PrefetchScalarGridSpec(
        num_scalar_prefetch=1,
        grid=(nb, n_pages // pps),
        in_specs=[pl.BlockSpec((None, 64, LANES), per_b),
                  pl.BlockSpec((None, 64, LANES), per_b),
                  pl.BlockSpec((None, LANES, LANES), per_b)] + _page_specs((PAGE_SIZE, IDX_DIM), pps),
        out_specs=[pl.BlockSpec((None, 8, scw), per_b),
                   pl.BlockSpec((None, 8, LANES), per_b),
                   pl.BlockSpec((None, 2, 8, LANES), lambda b, j, pt: (b, 0, 0, 0))],
    )
    return pl.pallas_call(
        kern,
        grid_spec=grid_spec,
        out_shape=[jax.ShapeDtypeStruct((nb, 8, scw), F32),
                   jax.ShapeDtypeStruct((nb, 8, LANES), F32),
                   jax.ShapeDtypeStruct((nb, 2, 8, LANES), I32)],
        compiler_params=_cparams(2),
        name="dsa_sample_scores",
    )(page_table, iq_s, iw_s, ik_new, *([cache_idx] * pps))


def _dsa_sample_attn_kernel(pt_ref, q_ref, sc_ref, thr_ref, jlim_ref, kn_ref, vn_ref, *rest, pps, past_len):
    pages = rest[:pps]
    o_ref, kbuf_ref, vbuf_ref, m_ref, l_ref, acc_ref = rest[pps:]
    j = pl.program_id(1)
    thr = thr_ref[...]
    jpos = jlim_ref[0]
    jneg = jlim_ref[1]
    lane = lax.broadcasted_iota(I32, (8, LANES), 1)
    rows_per_token = 2 * A_KV_HEADS

    @pl.when(j == 0)
    def _():
        _flash_init(m_ref, l_ref, acc_ref)

    def bias_at(c0):
        return _mask_bias(_selected(sc_ref[:, pl.ds(c0, LANES)], thr, jpos, jneg, lane + c0))

    for k in range(pps):
        for g in range(A_KV_HEADS):
            rows = pl.ds(k * PAGE_SIZE, PAGE_SIZE)
            kbuf_ref[g, rows, :] = pages[k][pl.ds(2 * g, PAGE_SIZE, stride=rows_per_token), :].astype(BF16)
            vbuf_ref[g, rows, :] = pages[k][pl.ds(2 * g + 1, PAGE_SIZE, stride=rows_per_token), :].astype(BF16)
    bias = [bias_at(pl.multiple_of((j * pps + k) * PAGE_SIZE, PAGE_SIZE)) for k in range(pps)]
    for g in range(A_KV_HEADS):
        _flash_step(q_ref[g], kbuf_ref[g], vbuf_ref[g], bias, m_ref, l_ref, acc_ref, g)

    @pl.when(j == pl.num_programs(1) - 1)
    def _():
        bias_new = [bias_at(past_len)]
        for g in range(A_KV_HEADS):
            _flash_step(q_ref[g], kn_ref[g], vn_ref[g], bias_new, m_ref, l_ref, acc_ref, g)
            o_ref[g] = acc_ref[g] / l_ref[g]


def _dsa_sample_attn(page_table, cache_kv, q_s, scores, thr, jlim, k_new, v_new, pps=8):
    nb, n_pages = page_table.shape
    pps = min(pps, n_pages)
    past_len = n_pages * PAGE_SIZE
    scw = scores.shape[-1]
    per_b3 = lambda b, j, pt: (b, 0, 0)
    per_b4 = lambda b, j, pt: (b, 0, 0, 0)
    page_rows = PAGE_SIZE * 2 * A_KV_HEADS
    kern = functools.partial(_dsa_sample_attn_kernel, pps=pps, past_len=past_len)
    grid_spec = pltpu.PrefetchScalarGridSpec(
        num_scalar_prefetch=1,
        grid=(nb, n_pages // pps),
        in_specs=[pl.BlockSpec((None, A_KV_HEADS, 8, HEAD_DIM), per_b4),
                  pl.BlockSpec((None, 8, scw), per_b3),
                  pl.BlockSpec((None, 8, LANES), per_b3),
                  pl.BlockSpec((None, 2, 8, LANES), per_b4),
                  pl.BlockSpec((None, A_KV_HEADS, LANES, HEAD_DIM), per_b4),
                  pl.BlockSpec((None, A_KV_HEADS, LANES, HEAD_DIM), per_b4)]
        + _page_specs((page_rows, HEAD_DIM), pps),
        out_specs=pl.BlockSpec((None, A_KV_HEADS, 8, HEAD_DIM), per_b4),
        scratch_shapes=[pltpu.VMEM((A_KV_HEADS, pps * PAGE_SIZE, HEAD_DIM), BF16),
                        pltpu.VMEM((A_KV_HEADS, pps * PAGE_SIZE, HEAD_DIM), BF16),
                        pltpu.VMEM((A_KV_HEADS, 8, LANES), F32),
                        pltpu.VMEM((A_KV_HEADS, 8, LANES), F32),
                        pltpu.VMEM((A_KV_HEADS, 8, HEAD_DIM), F32)],
    )
    return pl.pallas_call(
        kern,
        grid_spec=grid_spec,
        out_shape=jax.ShapeDtypeStruct((nb, A_KV_HEADS, 8, HEAD_DIM), F32),
        compiler_params=_cparams(2),
        name="dsa_sample_attn",
    )(page_table, q_s, scores, thr, jlim, k_new, v_new,
      *([cache_kv.reshape(cache_kv.shape[0], page_rows, HEAD_DIM)] * pps))


def _pad_keys(a, axis):
    pad = [(0, 0)] * a.ndim
    pad[axis] = (0, LANES - a.shape[axis])
    return jnp.pad(a, pad)


def _dsa_sample_layer(x, cache_kv, cache_idx, page_table, norm_g, w_in, q_g, k_g, w_o):
    nb, nt, _ = x.shape
    assert nt == 4
    n = nb * nt
    past_len = page_table.shape[1] * PAGE_SIZE
    pos = jnp.tile(past_len + jnp.arange(nt), nb)
    xf = x.reshape(n, D_MODEL)
    q, kv, k_hm, v_hm, iq, ikw, ikp, z = _proj_a(
        xf, norm_g, _layout_w_a(w_in), q_g, k_g, _rope_tables(pos, ROT_DIM), _rope_tables(pos, IDX_ROT), n)
    ga = N_HEADS // A_KV_HEADS
    iq_s = iq.reshape(nb, nt, IDX_HEADS, LANES).transpose(0, 2, 1, 3)
    iq_s = jnp.concatenate([iq_s, iq_s], axis=2).reshape(nb, 64, LANES)
    iw = ikw[:, IDX_DIM:IDX_DIM + IDX_HEADS].reshape(nb, nt, IDX_HEADS).transpose(0, 2, 1)
    iw_s = jnp.broadcast_to(jnp.concatenate([iw, iw], axis=2).reshape(nb, 64, 1), (nb, 64, LANES))
    ik_new = _pad_keys(ikp.reshape(nb, nt, LANES), 1)
    q_s = q.reshape(nb, nt, A_KV_HEADS, ga, HEAD_DIM).transpose(0, 2, 3, 1, 4).reshape(nb, A_KV_HEADS, 8, HEAD_DIM)
    k_new = _pad_keys(k_hm.reshape(A_KV_HEADS, nb, nt, HEAD_DIM).transpose(1, 0, 2, 3), 2)
    v_new = _pad_keys(v_hm.reshape(A_KV_HEADS, nb, nt, HEAD_DIM).transpose(1, 0, 2, 3), 2)
    n_keep = min(TOPK_MAX, (past_len + nt) // 4)
    scores, thr, jlim = _dsa_sample_scores(page_table, cache_idx, iq_s, iw_s, ik_new, n_keep)
    o_s = _dsa_sample_attn(page_table, cache_kv, q_s, scores, thr, jlim, k_new, v_new)
    o = o_s.reshape(nb, A_KV_HEADS, ga, nt, HEAD_DIM).transpose(0, 3, 1, 2, 4).reshape(n, WIDTH)
    y = _out_proj(xf, o, z, w_o, n)
    return (y.reshape(nb, nt, D_MODEL), kv.reshape(nb, nt, A_KV_HEADS, 2, HEAD_DIM),
            ikw[:, :IDX_DIM].reshape(nb, nt, IDX_DIM))


CHUNKS_PER_PAGE = PAGE_SIZE // CMP_STRIDE


SLABS_B = 2 * B_KV_HEADS
SLAB_COLS = CMP_STRIDE * HEAD_DIM


def _layout_cmp_slabs(w_cmp, pe_cmp):
    r = CMP_BLOCK // CMP_STRIDE
    w = w_cmp.reshape(r, CMP_STRIDE, 2, HEAD_DIM, HEAD_DIM).transpose(0, 2, 1, 3, 4)
    pe = pe_cmp.reshape(r, CMP_STRIDE, 2, HEAD_DIM).transpose(0, 2, 1, 3)
    return w.reshape(r, 2, SLAB_COLS, HEAD_DIM).astype(BF16), pe.reshape(r, 2, 1, SLAB_COLS)


def _cmp_paged_kernel(pt_ref, xn_ref, pe_ref, w_ref, *rest, pps):
    pages = rest[:pps]
    p0_ref, p1_ref, pn_ref, xs_ref = rest[pps:]
    for k in range(pps):
        for l in range(CMP_STRIDE):
            for s in range(SLABS_B):
                xs_ref[s, k * CHUNKS_PER_PAGE:(k + 1) * CHUNKS_PER_PAGE, l * HEAD_DIM:(l + 1) * HEAD_DIM] = (
                    pages[k][pl.ds(l * SLABS_B + s, CHUNKS_PER_PAGE, stride=CMP_STRIDE * SLABS_B), :])
    for s in range(SLABS_B):
        c = s % 2
        x = xs_ref[s]
        cols = slice(s * HEAD_DIM, (s + 1) * HEAD_DIM)
        p0_ref[:, cols] = jnp.dot((x + pe_ref[0, c]).astype(BF16), w_ref[0, c], preferred_element_type=F32)
        p1_ref[:, cols] = jnp.dot((x + pe_ref[1, c]).astype(BF16), w_ref[1, c], preferred_element_type=F32)
        pn_ref[:, cols] = jnp.dot((xn_ref[s] + pe_ref[1, c]).astype(BF16), w_ref[1, c],
                                  preferred_element_type=F32)


def _cmp_paged(page_table, cache_cmp, x_new, w, pe, pps=32):
    nb, n_pages = page_table.shape
    pps = min(pps, n_pages)
    steps = n_pages // pps
    rows = pps * CHUNKS_PER_PAGE
    kern = functools.partial(_cmp_paged_kernel, pps=pps)
    grid_spec = pltpu.PrefetchScalarGridSpec(
        num_scalar_prefetch=1,
        grid=(nb, steps),
        in_specs=[pl.BlockSpec((None, SLABS_B, 8, SLAB_COLS), lambda b, j, pt: (b, 0, 0, 0)),
                  pl.BlockSpec((2, 2, 1, SLAB_COLS), lambda b, j, pt: (0, 0, 0, 0)),
                  pl.BlockSpec((2, 2, SLAB_COLS, HEAD_DIM), lambda b, j, pt: (0, 0, 0, 0))]
        + _page_specs((PAGE_SIZE * SLABS_B, HEAD_DIM), pps),
        out_specs=[pl.BlockSpec((rows, B_KVW), lambda b, j, pt: (b * steps + j, 0)),
                   pl.BlockSpec((rows, B_KVW), lambda b, j, pt: (b * steps + j, 0)),
                   pl.BlockSpec((None, 8, B_KVW), lambda b, j, pt: (b, 0, 0))],
        scratch_shapes=[pltpu.VMEM((SLABS_B, rows, SLAB_COLS), F32)],
    )
    n_chunks = n_pages * CHUNKS_PER_PAGE
    return pl.pallas_call(
        kern,
        grid_spec=grid_spec,
        out_shape=[jax.ShapeDtypeStruct((nb * n_chunks, B_KVW), F32),
                   jax.ShapeDtypeStruct((nb * n_chunks, B_KVW), F32),
                   jax.ShapeDtypeStruct((nb, 8, B_KVW), F32)],
        compiler_params=_cparams(2),
        name="cmp_paged",
    )(page_table, x_new, pe, w,
      *([cache_cmp.reshape(cache_cmp.shape[0], PAGE_SIZE * SLABS_B, HEAD_DIM)] * pps))


QROWS_B = GB * 4
EXPAND_COLS = 1024


def _nsa_sample_kernel(pt_ref, q_ref, gate_ref, p0_ref, p1_ref, pn_ref, ov_ref, ksn_ref, vsn_ref,
                       win_ref, kwn_ref, vwn_ref, *rest, pps, past_len, n_cmp, n_sel):
    pages = rest[:pps]
    (o_ref, kc_ref, vc_ref, imp_ref, bias_ref, oc_ref, ow_ref, kbuf_ref, vbuf_ref,
     m_ref, l_ref, acc_ref) = rest[pps:]
    j = pl.program_id(1)
    nc = p0_ref.shape[0]
    tok = _row_token((QROWS_B, LANES))
    lane = lax.broadcasted_iota(I32, (QROWS_B, LANES), 1)
    rows_per_token = 2 * B_KV_HEADS

    @pl.when(j == 0)
    def _():
        rown = lax.broadcasted_iota(I32, (nc, B_KVW), 0)
        kc = p0_ref[...] + jnp.where(rown == nc - 1, jnp.broadcast_to(pn_ref[0:1, :], (nc, B_KVW)),
                                     pltpu.roll(p1_ref[...], nc - 1, 0))
        for g in range(B_KV_HEADS):
            kc_ref[g] = kc[:, (2 * g) * LANES:(2 * g + 1) * LANES].astype(BF16)
            vc_ref[g] = kc[:, (2 * g + 1) * LANES:(2 * g + 2) * LANES].astype(BF16)
        nidx = lax.broadcasted_iota(I32, (QROWS_B, nc), 1)
        tpos_c = past_len + _row_token((QROWS_B, nc))
        cvalid = jnp.where(nidx < n_cmp, nidx * CMP_STRIDE + (CMP_BLOCK - 1), 2 ** 30) <= tpos_c
        n_lt = imp_ref.shape[1] // LANES
        imp_shape = (QROWS_B, imp_ref.shape[1])
        jblk = lax.broadcasted_iota(I32, imp_shape, 1)
        tpos_i = past_len + _row_token(imp_shape)
        cur = tpos_i >> 6
        forced = (jblk == 0) | (jblk == cur) | (jblk == cur - 1)
        admissible = jblk * SLC_BLOCK <= tpos_i
        wb = win_ref.shape[0] // rows_per_token
        widx = lax.broadcasted_iota(I32, (QROWS_B, wb), 1)
        bias_w = jnp.concatenate([_mask_bias((widx - wb) > (_row_token((QROWS_B, wb)) - WINDOW)),
                                  _mask_bias(lane <= tok)], axis=1)
        for g in range(B_KV_HEADS):
            q = q_ref[g]
            s = lax.dot_general(q, kc_ref[g], _NT, preferred_element_type=F32)
            s = jnp.where(cvalid, s, NEG_BIG)
            e = jnp.where(cvalid, jnp.exp2(s - jnp.max(s, axis=1, keepdims=True)), 0.0)
            l = jnp.sum(e, axis=1, keepdims=True)
            p = e / jnp.where(l > 0.0, l, 1.0)
            oc_ref[g] = jnp.dot(p.astype(BF16), vc_ref[g], preferred_element_type=F32)
            pg = p[0:8] + p[8:16]
            pg = pg + pltpu.roll(pg, 4, 0)
            imp = _dot_f32(jnp.concatenate([pg, pg], axis=0), ov_ref[...])
            imp_ref[g * QROWS_B:(g + 1) * QROWS_B, :] = jnp.where(
                forced, jnp.inf, jnp.where(admissible, imp, -jnp.inf))

            kw = jnp.concatenate([win_ref[pl.ds(2 * g, wb, stride=rows_per_token), :].astype(BF16), kwn_ref[g]],
                                 axis=0)
            vw = jnp.concatenate([win_ref[pl.ds(2 * g + 1, wb, stride=rows_per_token), :].astype(BF16),
                                  vwn_ref[g]], axis=0)
            sw = lax.dot_general(q, kw, _NT, preferred_element_type=F32) + bias_w
            pw = jnp.exp2(sw - jnp.max(sw, axis=1, keepdims=True))
            ow_ref[g] = (jnp.dot(pw.astype(BF16), vw, preferred_element_type=F32)
                         / jnp.sum(pw, axis=1, keepdims=True))

        n_rows = B_KV_HEADS * QROWS_B
        thr, jpos, jneg = _topk_threshold(imp_ref, 1, n_lt, n_sel)
        lane_a = lax.broadcasted_iota(I32, (n_rows, LANES), 1)
        selb = jnp.concatenate(
            [jnp.where(_selected(imp_ref[:, c * LANES:(c + 1) * LANES], thr, jpos, jneg, lane_a + c * LANES),
                       1.0, 0.0) for c in range(n_lt)], axis=1).astype(BF16)
        total = bias_ref.shape[2]
        c0 = 0
        while c0 < total:
            w = min(EXPAND_COLS, total - c0)
            blk_of_key = (lax.broadcasted_iota(I32, (imp_ref.shape[1], w), 1) + c0) >> 6
            ex = jnp.where(blk_of_key == lax.broadcasted_iota(I32, (imp_ref.shape[1], w), 0), 1.0, 0.0)
            hit = jnp.dot(selb, ex.astype(BF16), preferred_element_type=F32)
            kpos = lax.broadcasted_iota(I32, (n_rows, w), 1) + c0
            visible = jnp.where(kpos <= past_len + _row_token((n_rows, w)), hit, 0.0) > 0.5
            bias = _mask_bias(visible)
            for g in range(B_KV_HEADS):
                bias_ref[g, :, c0:c0 + w] = bias[g * QROWS_B:(g + 1) * QROWS_B]
            c0 += w
        _flash_init(m_ref, l_ref, acc_ref)

    for k in range(pps):
        for g in range(B_KV_HEADS):
            rows = pl.ds(k * PAGE_SIZE, PAGE_SIZE)
            kbuf_ref[g, rows, :] = pages[k][pl.ds(2 * g, PAGE_SIZE, stride=rows_per_token), :].astype(BF16)
            vbuf_ref[g, rows, :] = pages[k][pl.ds(2 * g + 1, PAGE_SIZE, stride=rows_per_token), :].astype(BF16)
    for g in range(B_KV_HEADS):
        bias = [bias_ref[g, :, pl.ds(pl.multiple_of((j * pps + k) * PAGE_SIZE, PAGE_SIZE), PAGE_SIZE)]
                for k in range(pps)]
        _flash_step(q_ref[g], kbuf_ref[g], vbuf_ref[g], bias, m_ref, l_ref, acc_ref, g)

    @pl.when(j == pl.num_programs(1) - 1)
    def _():
        for g in range(B_KV_HEADS):
            _flash_step(q_ref[g], ksn_ref[g], vsn_ref[g], [bias_ref[g, :, past_len:past_len + LANES]],
                        m_ref, l_ref, acc_ref, g)
            o_ref[g] = (gate_ref[g, 0] * oc_ref[g] + gate_ref[g, 1] * (acc_ref[g] / l_ref[g])
                        + gate_ref[g, 2] * ow_ref[g])


def _nsa_sample(page_table, cache_slc, state_win, q_s, gates_s, p0, p1, pn, ks_new, vs_new, kw_new, vw_new,
                pps=16):
    nb, n_pages = page_table.shape
    pps = min(pps, n_pages)
    past_len = n_pages * PAGE_SIZE
    total = past_len + 4
    ncz = -(-total // CMP_STRIDE)
    nc = ncz - 1
    n_cmp = ncz - CMP_BLOCK // CMP_STRIDE + 1
    assert nc == n_pages * CHUNKS_PER_PAGE and n_cmp == nc
    n_slc = -(-total // SLC_BLOCK)
    imp_w = -(-n_slc // LANES) * LANES
    ov = _overlap_matrix(nc, n_cmp, imp_w)
    wb = state_win.shape[1]
    slabs = 2 * B_KV_HEADS
    per_b3 = lambda b, j, pt: (b, 0, 0)
    per_b4 = lambda b, j, pt: (b, 0, 0, 0)
    once = pl.Buffered(1)
    kern = functools.partial(_nsa_sample_kernel, pps=pps, past_len=past_len, n_cmp=n_cmp,
                             n_sel=min(SLC_TOPN, n_slc))
    new_spec = pl.BlockSpec((None, B_KV_HEADS, LANES, HEAD_DIM), per_b4)
    grid_spec = pltpu.PrefetchScalarGridSpec(
        num_scalar_prefetch=1,
        grid=(nb, n_pages // pps),
        in_specs=[pl.BlockSpec((None, B_KV_HEADS, QROWS_B, HEAD_DIM), per_b4),
                  pl.BlockSpec((None, B_KV_HEADS, 3, QROWS_B, HEAD_DIM), lambda b, j, pt: (b, 0, 0, 0, 0)),
                  pl.BlockSpec((nc, B_KVW), lambda b, j, pt: (b, 0)),
                  pl.BlockSpec((nc, B_KVW), lambda b, j, pt: (b, 0)),
                  pl.BlockSpec((None, 8, B_KVW), per_b3),
                  pl.BlockSpec((nc, imp_w), lambda b, j, pt: (0, 0), pipeline_mode=once),
                  new_spec, new_spec,
                  pl.BlockSpec((None, wb * slabs, HEAD_DIM), per_b3),
                  new_spec, new_spec]
        + _page_specs((PAGE_SIZE * slabs, HEAD_DIM), pps),
        out_specs=pl.BlockSpec((None, B_KV_HEADS, QROWS_B, HEAD_DIM), per_b4),
        scratch_shapes=[pltpu.VMEM((B_KV_HEADS, nc, HEAD_DIM), BF16),
                        pltpu.VMEM((B_KV_HEADS, nc, HEAD_DIM), BF16),
                        pltpu.VMEM((B_KV_HEADS * QROWS_B, imp_w), F32),
                        pltpu.VMEM((B_KV_HEADS, QROWS_B, past_len + LANES), F32),
                        pltpu.VMEM((B_KV_HEADS, QROWS_B, HEAD_DIM), F32),
                        pltpu.VMEM((B_KV_HEADS, QROWS_B, HEAD_DIM), F32),
                        pltpu.VMEM((B_KV_HEADS, pps * PAGE_SIZE, HEAD_DIM), BF16),
                        pltpu.VMEM((B_KV_HEADS, pps * PAGE_SIZE, HEAD_DIM), BF16),
                        pltpu.VMEM((B_KV_HEADS, QROWS_B, LANES), F32),
                        pltpu.VMEM((B_KV_HEADS, QROWS_B, LANES), F32),
                        pltpu.VMEM((B_KV_HEADS, QROWS_B, HEAD_DIM), F32)],
    )
    return pl.pallas_call(
        kern,
        grid_spec=grid_spec,
        out_shape=jax.ShapeDtypeStruct((nb, B_KV_HEADS, QROWS_B, HEAD_DIM), F32),
        compiler_params=_cparams(2),
        name="nsa_sample",
    )(page_table, q_s, gates_s, p0, p1, pn, ov, ks_new, vs_new,
      state_win.reshape(nb, wb * slabs, HEAD_DIM), kw_new, vw_new,
      *([cache_slc.reshape(cache_slc.shape[0], PAGE_SIZE * slabs, HEAD_DIM)] * pps))


def _nsa_sample_layer(x, cache_cmp, cache_slc, state_win, page_table, norm_g, w_in, q_g, k_g,
                      w_cmp, pe_cmp, w_o):
    nb, nt, _ = x.shape
    assert nt == 4
    n = nb * nt
    past_len = page_table.shape[1] * PAGE_SIZE
    pos = jnp.tile(past_len + jnp.arange(nt), nb)
    xf = x.reshape(n, D_MODEL)
    q, kvc, kvs, kvw, ks, vs, kw, vw, gates, z = _proj_b(
        xf, norm_g, _layout_w_b(w_in), q_g, k_g, _rope_tables(pos, ROT_DIM), n)
    wc, pe = _layout_cmp_slabs(w_cmp, pe_cmp)
    x_new = kvc.reshape(nb, nt, SLABS_B, HEAD_DIM).transpose(0, 2, 1, 3).reshape(nb, SLABS_B, 1, nt * HEAD_DIM)
    x_new = jnp.pad(x_new, ((0, 0), (0, 0), (0, 7), (0, SLAB_COLS - nt * HEAD_DIM)))
    p0, p1, pn = _cmp_paged(page_table, cache_cmp, x_new, wc, pe)
    q_s = q.reshape(nb, nt, B_KV_HEADS, GB, HEAD_DIM).transpose(0, 2, 3, 1, 4).reshape(nb, B_KV_HEADS, QROWS_B, HEAD_DIM)
    gt = gates[:, :3 * N_HEADS].reshape(nb, nt, B_KV_HEADS, GB, 3).transpose(0, 2, 4, 3, 1)
    gates_s = jnp.broadcast_to(gt.reshape(nb, B_KV_HEADS, 3, QROWS_B, 1), (nb, B_KV_HEADS, 3, QROWS_B, HEAD_DIM))
    new_keys = lambda a: _pad_keys(a.reshape(B_KV_HEADS, nb, nt, HEAD_DIM).transpose(1, 0, 2, 3), 2)
    o_s = _nsa_sample(page_table, cache_slc, state_win, q_s, gates_s, p0, p1, pn,
                      new_keys(ks), new_keys(vs), new_keys(kw), new_keys(vw))
    o = o_s.reshape(nb, B_KV_HEADS, GB, nt, HEAD_DIM).transpose(0, 3, 1, 2, 4).reshape(n, WIDTH)
    y = _out_proj(xf, o, z, w_o, n)
    shp = (nb, nt, B_KV_HEADS, 2, HEAD_DIM)
    win_out = jnp.concatenate([state_win, kvw.reshape(shp)], axis=1)[:, nt:]
    return y.reshape(nb, nt, D_MODEL), kvc.reshape(shp), kvs.reshape(shp), win_out


def kernel(x_prompt, x_sample, cache_a_kv, cache_a_idx, cache_b_cmp_kv, cache_b_slc_kv, state_b_win_kv,
           page_table, a_norm, a_w_in, a_q_norm, a_k_norm, a_w_o, b_norm, b_w_in, b_q_norm, b_k_norm,
           b_cmp_w, b_cmp_pe, b_w_o):
    yp, a_kv_p, a_idx_p = _dsa_prompt_layer(x_prompt, a_norm, a_w_in, a_q_norm, a_k_norm, a_w_o)
    yp, b_cmp_p, b_slc_p, b_win_p = _nsa_prompt_layer(yp, b_norm, b_w_in, b_q_norm, b_k_norm,
                                                      b_cmp_w, b_cmp_pe, b_w_o)
    ys, a_kv_s, a_idx_s = _dsa_sample_layer(x_sample, cache_a_kv, cache_a_idx, page_table,
                                            a_norm, a_w_in, a_q_norm, a_k_norm, a_w_o)
    ys, b_cmp_s, b_slc_s, b_win_s = _nsa_sample_layer(ys, cache_b_cmp_kv, cache_b_slc_kv, state_b_win_kv,
                                                      page_table, b_norm, b_w_in, b_q_norm, b_k_norm,
                                                      b_cmp_w, b_cmp_pe, b_w_o)
    return (yp, ys, a_kv_p, a_idx_p, a_kv_s, a_idx_s, b_cmp_p, b_slc_p, b_win_p, b_cmp_s, b_slc_s, b_win_s)
```

```python
import functools

import numpy as np
import jax
import jax.numpy as jnp
from jax import lax
from jax.experimental import pallas as pl
from jax.experimental.pallas import tpu as pltpu

F32 = jnp.float32
BF16 = jnp.bfloat16
I32 = jnp.int32

D_MODEL = 1024
N_HEADS = 8
HEAD_DIM = 128
WIDTH = N_HEADS * HEAD_DIM
ROT_DIM = HEAD_DIM // 4
ROPE_THETA = 500000.0
EPS = 1e-6
ATTN_SCALE = HEAD_DIM ** -0.5
Q_SCALE = ATTN_SCALE * 1.4426950408889634
PAGE_SIZE = 128
A_KV_HEADS = 4
IDX_HEADS = 8
IDX_DIM = 64
IDX_ROT = IDX_DIM // 4
TOPK_MAX = 256
B_KV_HEADS = 2
GB = N_HEADS // B_KV_HEADS
CMP_BLOCK = 32
CMP_STRIDE = 16
SLC_BLOCK = 64
SLC_TOPN = 16
WINDOW = 512

LANES = 128
NEG_BIG = -1e30
KEY_NEG_INF = -2139095041
VMEM_LIMIT = 56 * 1024 * 1024

_NT = (((1,), (1,)), ((), ()))


def _cparams(n_grid_dims):
    return pltpu.CompilerParams(
        dimension_semantics=("arbitrary",) * n_grid_dims,
        vmem_limit_bytes=VMEM_LIMIT)


def _rope_tables(pos, rot_dim):
    half = rot_dim // 2
    freq = ROPE_THETA ** (-jnp.arange(half, dtype=F32) / half)
    ang = pos.astype(F32)[:, None] * freq
    cos, sin = jnp.cos(ang), jnp.sin(ang)
    t = pos.shape[0]
    one = jnp.ones((t, LANES - rot_dim), F32)
    zero = jnp.zeros((t, LANES - rot_dim), F32)
    zh = jnp.zeros((t, half), F32)
    c = jnp.concatenate([cos, cos, one], axis=1)
    s1 = jnp.concatenate([-sin, zh, zero], axis=1)
    s2 = jnp.concatenate([zh, sin, zero], axis=1)
    return jnp.stack([c, s1, s2])


def _rope(t, tab_ref, half):
    return (t * tab_ref[0] + pltpu.roll(t, LANES - half, 1) * tab_ref[1]
            + pltpu.roll(t, half, 1) * tab_ref[2])


def _head_norm(t, g):
    return t * lax.rsqrt(jnp.mean(t * t, axis=-1, keepdims=True) + EPS) * g


A_COLS = 4 * WIDTH + LANES


def _layout_w_a(w_in):
    o = 0
    wq = w_in[:, o:o + WIDTH]; o += WIDTH
    wkv = w_in[:, o:o + 2 * A_KV_HEADS * HEAD_DIM]; o += 2 * A_KV_HEADS * HEAD_DIM
    wiq = w_in[:, o:o + IDX_HEADS * IDX_DIM]; o += IDX_HEADS * IDX_DIM
    wik = w_in[:, o:o + IDX_DIM]; o += IDX_DIM
    wiw = w_in[:, o:o + IDX_HEADS]; o += IDX_HEADS
    wz = w_in[:, o:o + WIDTH]
    d = w_in.shape[0]
    wiq = jnp.pad(wiq.reshape(d, IDX_HEADS, IDX_DIM), ((0, 0), (0, 0), (0, LANES - IDX_DIM))).reshape(d, -1)
    wikw = jnp.concatenate([wik, wiw, jnp.zeros((d, LANES - IDX_DIM - IDX_HEADS), w_in.dtype)], axis=1)
    return jnp.concatenate([wq, wkv, wiq, wikw, wz], axis=1).astype(BF16)


def _proj_a_kernel(x_ref, g_ref, w_ref, qg_ref, kg_ref, rt_ref, it_ref,
                   q_out, kv_out, k_out, v_out, iq_out, ikw_out, ikp_out, z_out, xn_ref):
    x = x_ref[...]
    ms = jnp.mean(x * x, axis=-1, keepdims=True)
    xn_ref[...] = (x * lax.rsqrt(ms + EPS) * g_ref[...]).astype(BF16)

    def seg(c0):
        return jnp.dot(xn_ref[...], w_ref[:, c0:c0 + LANES], preferred_element_type=F32)

    qg = qg_ref[...]
    kg = kg_ref[...]
    for h in range(N_HEADS):
        t = _rope(_head_norm(seg(h * LANES), qg), rt_ref, ROT_DIM // 2)
        q_out[:, h * LANES:(h + 1) * LANES] = (t * Q_SCALE).astype(BF16)
    base = WIDTH
    for g in range(A_KV_HEADS):
        k = _rope(_head_norm(seg(base + (2 * g) * LANES), kg), rt_ref, ROT_DIM // 2)
        v = seg(base + (2 * g + 1) * LANES)
        tm = k.shape[0]
        kv_out[pl.ds(2 * g, tm, stride=2 * A_KV_HEADS), :] = k
        kv_out[pl.ds(2 * g + 1, tm, stride=2 * A_KV_HEADS), :] = v
        k_out[g] = k.astype(BF16)
        v_out[g] = v.astype(BF16)
    base = 2 * WIDTH
    for h in range(IDX_HEADS):
        t = _rope(seg(base + h * LANES), it_ref, IDX_ROT // 2)
        iq_out[:, h * LANES:(h + 1) * LANES] = t.astype(BF16)
    base = 3 * WIDTH
    t = _rope(seg(base), it_ref, IDX_ROT // 2)
    lane = lax.broadcasted_iota(I32, t.shape, 1)
    w_scale = (IDX_HEADS * IDX_DIM) ** -0.5
    ikw_out[...] = jnp.where(lane < IDX_DIM, t, t * w_scale)
    ikp_out[...] = jnp.where(lane < IDX_DIM, t, 0.0).astype(BF16)
    base = 3 * WIDTH + LANES
    for h in range(N_HEADS):
        z_out[:, h * LANES:(h + 1) * LANES] = seg(base + h * LANES)


def _proj_a(x, norm_g, w_a, q_g, k_g, rope_tab, idx_tab, tm):
    n = x.shape[0]
    period = rope_tab.shape[1] // tm
    row = lambda i: (i, 0)
    const = lambda i: (0, 0)
    tab = lambda i: (0, i % period, 0)
    hm = lambda i: (0, i, 0)
    return pl.pallas_call(
        _proj_a_kernel,
        grid=(n // tm,),
        in_specs=[
            pl.BlockSpec((tm, D_MODEL), row),
            pl.BlockSpec((1, D_MODEL), const),
            pl.BlockSpec((D_MODEL, A_COLS), const),
            pl.BlockSpec((1, HEAD_DIM), const),
            pl.BlockSpec((1, HEAD_DIM), const),
            pl.BlockSpec((3, tm, LANES), tab),
            pl.BlockSpec((3, tm, LANES), tab),
        ],
        out_specs=[
            pl.BlockSpec((tm, WIDTH), row),
            pl.BlockSpec((tm * 2 * A_KV_HEADS, HEAD_DIM), row),
            pl.BlockSpec((A_KV_HEADS, tm, HEAD_DIM), hm),
            pl.BlockSpec((A_KV_HEADS, tm, HEAD_DIM), hm),
            pl.BlockSpec((tm, IDX_HEADS * LANES), row),
            pl.BlockSpec((tm, LANES), row),
            pl.BlockSpec((tm, LANES), row),
            pl.BlockSpec((tm, WIDTH), row),
        ],
        out_shape=[
            jax.ShapeDtypeStruct((n, WIDTH), BF16),
            jax.ShapeDtypeStruct((n * 2 * A_KV_HEADS, HEAD_DIM), F32),
            jax.ShapeDtypeStruct((A_KV_HEADS, n, HEAD_DIM), BF16),
            jax.ShapeDtypeStruct((A_KV_HEADS, n, HEAD_DIM), BF16),
            jax.ShapeDtypeStruct((n, IDX_HEADS * LANES), BF16),
            jax.ShapeDtypeStruct((n, LANES), F32),
            jax.ShapeDtypeStruct((n, LANES), BF16),
            jax.ShapeDtypeStruct((n, WIDTH), F32),
        ],
        scratch_shapes=[pltpu.VMEM((tm, D_MODEL), BF16)],
        compiler_params=_cparams(1),
        name="proj_a",
    )(x, norm_g.reshape(1, -1), w_a, q_g.reshape(1, -1), k_g.reshape(1, -1), rope_tab, idx_tab)


def _out_proj_kernel(x_ref, o_ref, z_ref, w_ref, y_ref):
    z = z_ref[...]
    h = (o_ref[...] * (z * jax.nn.sigmoid(z))).astype(BF16)
    y_ref[...] = x_ref[...] + jnp.dot(h, w_ref[...], preferred_element_type=F32)


def _out_proj(x, o, z, w_o, tm):
    n = x.shape[0]
    row = lambda i: (i, 0)
    return pl.pallas_call(
        _out_proj_kernel,
        grid=(n // tm,),
        in_specs=[pl.BlockSpec((tm, D_MODEL), row), pl.BlockSpec((tm, WIDTH), row),
                  pl.BlockSpec((tm, WIDTH), row), pl.BlockSpec((WIDTH, D_MODEL), lambda i: (0, 0))],
        out_specs=pl.BlockSpec((tm, D_MODEL), row),
        out_shape=jax.ShapeDtypeStruct((n, D_MODEL), F32),
        compiler_params=_cparams(1),
        name="out_proj",
    )(x, o, z, w_o.astype(BF16))


def _key_to_float(t):
    bits = t ^ ((t >> 31) & 0x7FFFFFFF)
    return pltpu.bitcast(bits, F32)


def _neg_zero(x):
    return (x == 0.0) & (pltpu.bitcast(x, I32) < 0)


WORD_BITS = 32


def _tie_words(sc_ref, n_groups, group, thr, r0):
    rc = thr.shape[0]
    n_words = -(-(sc_ref.shape[1] // LANES) // WORD_BITS)
    n_tiles = n_groups * group
    step = next(s for s in (16, 8, 4, 2, 1) if group % s == 0)
    pos, neg = [], []
    for w in range(n_words):
        trips = jnp.clip(n_tiles - w * WORD_BITS, 0, WORD_BITS) // step

        def body(q, acc, w=w):
            p, n = acc
            for u in range(step):
                b = q * step + u
                c0 = pl.multiple_of((w * WORD_BITS + b) * LANES, LANES)
                blk = sc_ref[r0:r0 + rc, pl.ds(c0, LANES)]
                bit = jnp.int32(1) << b
                tie = blk == thr
                nz = _neg_zero(blk)
                p = p | jnp.where(tie & ~nz, bit, 0)
                n = n | jnp.where(tie & nz, bit, 0)
            return p, n

        zero = jnp.zeros((rc, LANES), I32)
        p, n = lax.fori_loop(0, trips, body, (zero, zero))
        pos.append(p)
        neg.append(n)
    return pos, neg


def _lane_total(x):
    return jnp.broadcast_to(jnp.sum(x, axis=1, keepdims=True), x.shape)


def _tie_limit(words, quota, n_cols):
    rows = quota.shape[0]
    lane = lax.broadcasted_iota(I32, (rows, LANES), 1)

    def flagged_before(j):
        tile, lane_lim = j >> 7, j & (LANES - 1)
        total = jnp.zeros((rows, LANES), I32)
        for w, word in enumerate(words):
            m = jnp.clip(tile - w * WORD_BITS, 0, WORD_BITS)
            below = jnp.where(m >= WORD_BITS, -1, (jnp.int32(1) << jnp.minimum(m, WORD_BITS - 1)) - 1)
            total = total + lax.population_count(word & below)
            here = ((tile >> 5) == w) & (lane < lane_lim)
            total = total + jnp.where(here, (word >> (tile & (WORD_BITS - 1))) & 1, 0)
        return _lane_total(total)

    j = jnp.zeros((rows, LANES), I32)
    for bit in reversed(range(max(1, int(n_cols).bit_length()))):
        cand = j + (1 << bit)
        j = jnp.where(flagged_before(cand) <= quota, cand, j)
    return j


def _row_count(sc_ref, n_groups, group, row_chunk, pred, args):
    rows = sc_ref.shape[0]
    rc = min(row_chunk, rows)
    starts = list(range(0, rows, rc))
    parts = []
    for i in range(0, len(starts), 2):
        pair = starts[i:i + 2]
        a = [[x[r0:r0 + rc] if hasattr(x, "shape") and x.shape else x for x in args] for r0 in pair]

        def body(j, cs, pair=pair, a=a):
            cs = list(cs)
            for u in range(group):
                c0 = pl.multiple_of((j * group + u) * LANES, LANES)
                for n, r0 in enumerate(pair):
                    cs[n] = cs[n] + jnp.where(pred(sc_ref[r0:r0 + rc, pl.ds(c0, LANES)], c0, *a[n]), 1, 0)
            return tuple(cs)

        parts.extend(lax.fori_loop(0, n_groups, body, tuple(jnp.zeros((rc, LANES), I32) for _ in pair)))
    c = parts[0] if len(parts) == 1 else jnp.concatenate(parts, axis=0)
    return jnp.broadcast_to(jnp.sum(c, axis=1, keepdims=True), (rows, LANES))


def _topk_threshold(sc_ref, n_groups, group, k, row_chunk=64):
    rows = sc_ref.shape[0]
    count = functools.partial(_row_count, sc_ref, n_groups, group, row_chunk)

    def count_ge(t):
        return count(lambda blk, c0, thr: blk >= thr, [_key_to_float(t)])

    few_positive = count(lambda blk, c0: blk > 0.0, []) < k
    state = (jnp.int32(0), jnp.full((rows, LANES), -2 ** 31, I32), jnp.full((rows, LANES), 2 ** 30, I32))

    def unsettled(st):
        it, t, cnt = st
        open_rows = jnp.where(few_positive & (t == 0), 0, jnp.where(cnt != k, 1, 0))
        return jnp.logical_and(it < 32, jnp.max(open_rows) > 0)

    def bit_pair(st):
        it, t, cnt = st
        for step in range(2):
            cand = t + (jnp.int32(1) << (31 - it - step))
            c = count_ge(cand)
            ok = c >= k
            t = jnp.where(ok, cand, t)
            cnt = jnp.where(ok, c, cnt)
        return it + 2, t, cnt

    _, t, n_ge = lax.while_loop(unsettled, bit_pair, state)
    thr = _key_to_float(jnp.maximum(t, KEY_NEG_INF))
    short = thr == -jnp.inf
    excess = jnp.where(short, 0, n_ge - k)

    rc = min(row_chunk, rows)
    n_cols = sc_ref.shape[1]

    def chunk_limits(r0):
        rows_c = slice(r0, r0 + rc)

        def run():
            pos, neg = _tie_words(sc_ref, n_groups, group, thr[rows_c], r0)
            n_pos = _lane_total(sum(lax.population_count(w) for w in pos))
            n_neg = _lane_total(sum(lax.population_count(w) for w in neg))
            need = k - (n_ge[rows_c] - (n_pos + n_neg))
            need_pos = jnp.minimum(need, n_pos)
            need_neg = need - need_pos
            jn = lax.cond(jnp.max(need_neg) > 0, lambda: _tie_limit(neg, need_neg, n_cols),
                          lambda: jnp.zeros((rc, LANES), I32))
            return _tie_limit(pos, need_pos, n_cols), jn

        unbounded = lambda: (jnp.full((rc, LANES), 2 ** 30, I32), jnp.full((rc, LANES), 2 ** 30, I32))
        return lax.cond(jnp.max(excess[rows_c]) > 0, run, unbounded)

    limits = [chunk_limits(r0) for r0 in range(0, rows, rc)]
    cat = lambda parts: parts[0] if len(parts) == 1 else jnp.concatenate(parts, axis=0)
    jpos, jneg = cat([p for p, _ in limits]), cat([n for _, n in limits])
    return thr, jnp.where(short, 0, jpos), jnp.where(short, 0, jneg)


def _selected(s, thr, jpos, jneg, col):
    return (s > thr) | ((s == thr) & (col < jnp.where(_neg_zero(s), jneg, jpos)))


def _dsa_prompt_kernel(iq_ref, ikw_ref, ik_ref, q_ref, k_ref, v_ref, o_ref,
                       sc_ref, m_ref, l_ref, acc_ref, *, tq, tk, n_keep):
    i = pl.program_id(1)
    q0 = i * tq
    nkb = lax.div(q0 + tq + tk - 1, tk)
    tiles_per_kb = tk // LANES

    def score_body(j, carry):
        k0 = pl.multiple_of(j * tk, tk)
        ikb = ik_ref[pl.ds(k0, tk), :]
        acc = jnp.zeros((tq, tk), F32)
        for h in range(IDX_HEADS):
            lg = lax.dot_general(iq_ref[:, h * LANES:(h + 1) * LANES], ikb, _NT,
                                 preferred_element_type=F32)
            w = ikw_ref[:, IDX_DIM + h:IDX_DIM + h + 1]
            acc = acc + w * jnp.maximum(lg, 0.0)
        kpos = k0 + lax.broadcasted_iota(I32, (tq, tk), 1)
        tpos = q0 + lax.broadcasted_iota(I32, (tq, tk), 0)
        sc_ref[:, pl.ds(k0, tk)] = jnp.where(kpos <= tpos, acc, -jnp.inf)
        return carry

    lax.fori_loop(0, nkb, score_body, 0)

    thr, jpos, jneg = _topk_threshold(sc_ref, nkb, tiles_per_kb, n_keep)

    half = tq // 2
    lane = lax.broadcasted_iota(I32, (half, LANES), 1)

    def bias_body(j, carry):
        c0 = pl.multiple_of(j * LANES, LANES)
        for r0 in (0, half):
            sel = _selected(sc_ref[r0:r0 + half, pl.ds(c0, LANES)], thr[r0:r0 + half], jpos[r0:r0 + half],
                            jneg[r0:r0 + half], lane + c0)
            sc_ref[r0:r0 + half, pl.ds(c0, LANES)] = _mask_bias(sel)
        return carry

    lax.fori_loop(0, nkb * tiles_per_kb, bias_body, 0)

    _flash_init(m_ref, l_ref, acc_ref)

    def attn_body(j, carry):
        k0 = pl.multiple_of(j * tk, tk)
        bias = [sc_ref[:, pl.ds(k0 + t * LANES, LANES)] for t in range(tiles_per_kb)]
        for h in range(N_HEADS):
            g = h // (N_HEADS // A_KV_HEADS)
            _flash_step(q_ref[:, h * LANES:(h + 1) * LANES], k_ref[g, pl.ds(k0, tk), :],
                        v_ref[g, pl.ds(k0, tk), :], bias, m_ref, l_ref, acc_ref, h)
        return carry

    lax.fori_loop(0, nkb, attn_body, 0)
    for h in range(N_HEADS):
        o_ref[:, h * LANES:(h + 1) * LANES] = acc_ref[h] / l_ref[h]


def _dsa_prompt(iq, ikw, ikp, q, k_hm, v_hm, batch, seq, n_keep, tq=512, tk=512):
    tq = min(tq, seq)
    tk = min(tk, seq)
    nq = seq // tq
    n = batch * seq
    row = lambda b, i: (b * nq + i, 0)
    kern = functools.partial(_dsa_prompt_kernel, tq=tq, tk=tk, n_keep=n_keep)
    once = pl.Buffered(1)
    return pl.pallas_call(
        kern,
        grid=(batch, nq),
        in_specs=[
            pl.BlockSpec((tq, IDX_HEADS * LANES), row, pipeline_mode=once),
            pl.BlockSpec((tq, LANES), row),
            pl.BlockSpec((seq, LANES), lambda b, i: (b, 0), pipeline_mode=once),
            pl.BlockSpec((tq, WIDTH), row, pipeline_mode=once),
            pl.BlockSpec((A_KV_HEADS, seq, HEAD_DIM), lambda b, i: (0, b, 0), pipeline_mode=once),
            pl.BlockSpec((A_KV_HEADS, seq, HEAD_DIM), lambda b, i: (0, b, 0), pipeline_mode=once),
        ],
        out_specs=pl.BlockSpec((tq, WIDTH), row),
        out_shape=jax.ShapeDtypeStruct((n, WIDTH), F32),
        scratch_shapes=[
            pltpu.VMEM((tq, seq + LANES), F32),
            pltpu.VMEM((N_HEADS, tq, LANES), F32),
            pltpu.VMEM((N_HEADS, tq, LANES), F32),
            pltpu.VMEM((N_HEADS, tq, HEAD_DIM), F32),
        ],
        compiler_params=_cparams(2),
        name="dsa_prompt",
    )(iq, ikw, ikp, q, k_hm, v_hm)


def _dsa_prompt_layer(x, norm_g, w_in, q_g, k_g, w_o):
    b, s, _ = x.shape
    n = b * s
    pos = jnp.arange(s)
    tm = min(256, s)
    xf = x.reshape(n, D_MODEL)
    q, kv, k_hm, v_hm, iq, ikw, ikp, z = _proj_a(
        xf, norm_g, _layout_w_a(w_in), q_g, k_g, _rope_tables(pos, ROT_DIM), _rope_tables(pos, IDX_ROT), tm)
    o = _dsa_prompt(iq, ikw, ikp, q, k_hm, v_hm, b, s, min(TOPK_MAX, s // 4))
    y = _out_proj(xf, o, z, w_o, tm)
    return (y.reshape(b, s, D_MODEL), kv.reshape(b, s, A_KV_HEADS, 2, HEAD_DIM),
            ikw[:, :IDX_DIM].reshape(b, s, IDX_DIM))


B_KVW = 2 * B_KV_HEADS * HEAD_DIM
B_COLS = 2 * WIDTH + 3 * B_KVW + LANES


def _layout_w_b(w_in):
    o = WIDTH + 3 * B_KVW
    d = w_in.shape[0]
    wg = jnp.concatenate([w_in[:, o:o + 3 * N_HEADS], jnp.zeros((d, LANES - 3 * N_HEADS), w_in.dtype)], axis=1)
    return jnp.concatenate([w_in[:, :o], wg, w_in[:, o + 3 * N_HEADS:]], axis=1).astype(BF16)


def _proj_b_kernel(x_ref, g_ref, w_ref, qg_ref, kg_ref, rt_ref,
                   q_out, kvc_out, kvs_out, kvw_out, ks_out, vs_out, kw_out, vw_out, gate_out, z_out, xn_ref):
    x = x_ref[...]
    ms = jnp.mean(x * x, axis=-1, keepdims=True)
    xn_ref[...] = (x * lax.rsqrt(ms + EPS) * g_ref[...]).astype(BF16)

    def seg(c0):
        return jnp.dot(xn_ref[...], w_ref[:, c0:c0 + LANES], preferred_element_type=F32)

    qg = qg_ref[...]
    for h in range(N_HEADS):
        t = _rope(_head_norm(seg(h * LANES), qg), rt_ref, ROT_DIM // 2)
        q_out[:, h * LANES:(h + 1) * LANES] = (t * Q_SCALE).astype(BF16)
    branch_out = ((kvc_out, None, None), (kvs_out, ks_out, vs_out), (kvw_out, kw_out, vw_out))
    for br, (kv_out, k_out, v_out) in enumerate(branch_out):
        base = WIDTH + br * B_KVW
        kg = kg_ref[br:br + 1, :]
        for g in range(B_KV_HEADS):
            k = _rope(_head_norm(seg(base + (2 * g) * LANES), kg), rt_ref, ROT_DIM // 2)
            v = seg(base + (2 * g + 1) * LANES)
            tm = k.shape[0]
            kv_out[pl.ds(2 * g, tm, stride=2 * B_KV_HEADS), :] = k
            kv_out[pl.ds(2 * g + 1, tm, stride=2 * B_KV_HEADS), :] = v
            if k_out is not None:
                k_out[g] = k.astype(BF16)
                v_out[g] = v.astype(BF16)
    base = WIDTH + 3 * B_KVW
    gate_out[...] = jax.nn.sigmoid(seg(base))
    base += LANES
    for h in range(N_HEADS):
        z_out[:, h * LANES:(h + 1) * LANES] = seg(base + h * LANES)


def _proj_b(x, norm_g, w_b, q_g, k_g, rope_tab, tm):
    n = x.shape[0]
    period = rope_tab.shape[1] // tm
    row = lambda i: (i, 0)
    const = lambda i: (0, 0)
    hm = lambda i: (0, i, 0)
    kv_spec = pl.BlockSpec((tm * 2 * B_KV_HEADS, HEAD_DIM), row)
    hm_spec = pl.BlockSpec((B_KV_HEADS, tm, HEAD_DIM), hm)
    kv_shape = jax.ShapeDtypeStruct((n * 2 * B_KV_HEADS, HEAD_DIM), F32)
    hm_shape = jax.ShapeDtypeStruct((B_KV_HEADS, n, HEAD_DIM), BF16)
    return pl.pallas_call(
        _proj_b_kernel,
        grid=(n // tm,),
        in_specs=[
            pl.BlockSpec((tm, D_MODEL), row),
            pl.BlockSpec((1, D_MODEL), const),
            pl.BlockSpec((D_MODEL, B_COLS), const),
            pl.BlockSpec((1, HEAD_DIM), const),
            pl.BlockSpec((3, HEAD_DIM), const),
            pl.BlockSpec((3, tm, LANES), lambda i: (0, i % period, 0)),
        ],
        out_specs=[pl.BlockSpec((tm, WIDTH), row), kv_spec, kv_spec, kv_spec,
                   hm_spec, hm_spec, hm_spec, hm_spec,
                   pl.BlockSpec((tm, LANES), row), pl.BlockSpec((tm, WIDTH), row)],
        out_shape=[jax.ShapeDtypeStruct((n, WIDTH), BF16), kv_shape, kv_shape, kv_shape,
                   hm_shape, hm_shape, hm_shape, hm_shape,
                   jax.ShapeDtypeStruct((n, LANES), F32), jax.ShapeDtypeStruct((n, WIDTH), F32)],
        scratch_shapes=[pltpu.VMEM((tm, D_MODEL), BF16)],
        compiler_params=_cparams(1),
        name="proj_b",
    )(x, norm_g.reshape(1, -1), w_b, q_g.reshape(1, -1), k_g, rope_tab)


CHUNK_COLS = CMP_STRIDE * B_KVW


def _layout_cmp(w_cmp, pe_cmp):
    r = CMP_BLOCK // CMP_STRIDE
    eye_g = jnp.eye(B_KV_HEADS, dtype=w_cmp.dtype)
    eye_c = jnp.eye(2, dtype=w_cmp.dtype)
    wj = w_cmp.reshape(r, CMP_STRIDE, 2, HEAD_DIM, HEAD_DIM)
    w = jnp.einsum('jlcde,gh,ck->jlgcdhke', wj, eye_g, eye_c).reshape(r, CHUNK_COLS, B_KVW)
    pe = jnp.broadcast_to(pe_cmp.reshape(r, CMP_STRIDE, 1, 2, HEAD_DIM),
                          (r, CMP_STRIDE, B_KV_HEADS, 2, HEAD_DIM)).reshape(r, 1, CHUNK_COLS)
    return w.astype(BF16), pe


def _cmp_mm_kernel(x_ref, pe_ref, w_ref, p0_ref, p1_ref):
    x = x_ref[...]
    p0_ref[...] = jnp.dot((x + pe_ref[0]).astype(BF16), w_ref[0], preferred_element_type=F32)
    p1_ref[...] = jnp.dot((x + pe_ref[1]).astype(BF16), w_ref[1], preferred_element_type=F32)


def _cmp_mm(x, w, pe, tr):
    r = x.shape[0]
    row = lambda i: (i, 0)
    once = pl.Buffered(1)
    return pl.pallas_call(
        _cmp_mm_kernel,
        grid=(r // tr,),
        in_specs=[pl.BlockSpec((tr, CHUNK_COLS), row),
                  pl.BlockSpec((2, 1, CHUNK_COLS), lambda i: (0, 0, 0)),
                  pl.BlockSpec((2, CHUNK_COLS, B_KVW), lambda i: (0, 0, 0), pipeline_mode=once)],
        out_specs=[pl.BlockSpec((tr, B_KVW), row), pl.BlockSpec((tr, B_KVW), row)],
        out_shape=[jax.ShapeDtypeStruct((r, B_KVW), F32), jax.ShapeDtypeStruct((r, B_KVW), F32)],
        compiler_params=_cparams(1),
        name="cmp_mm",
    )(x, pe, w)


def _overlap_matrix(n_rows, n_cmp, n_cols):
    i = np.arange(n_rows)[:, None]
    j = np.arange(n_cols)[None, :]
    lo = np.maximum(i * CMP_STRIDE, j * SLC_BLOCK)
    hi = np.minimum(i * CMP_STRIDE + CMP_BLOCK, (j + 1) * SLC_BLOCK)
    ov = np.maximum(hi - lo, 0) / CMP_STRIDE
    ov = np.where(i < n_cmp, ov, 0.0)
    return jnp.asarray(ov, dtype=BF16)


def _dot_f32(a, b_bf16):
    hi = a.astype(BF16)
    r1 = a - hi.astype(F32)
    mid = r1.astype(BF16)
    lo = (r1 - mid.astype(F32)).astype(BF16)
    d = lambda p: jnp.dot(p, b_bf16, preferred_element_type=F32)
    return d(hi) + d(mid) + d(lo)


def _mask_bias(valid):
    return jnp.where(valid, 0.0, NEG_BIG)


def _flash_step(q, k, v, bias_tiles, m_ref, l_ref, acc_ref, h):
    s = lax.dot_general(q, k, _NT, preferred_element_type=F32)
    st = [s[:, t * LANES:(t + 1) * LANES] + b for t, b in enumerate(bias_tiles)]
    rows = s.shape[0]
    mt = functools.reduce(jnp.maximum, st)
    m_prev = m_ref[h]
    m_new = jnp.maximum(m_prev, jnp.broadcast_to(jnp.max(mt, axis=1, keepdims=True), (rows, LANES)))
    alpha = jnp.exp2(m_prev - m_new)
    pt = [jnp.exp2(x - m_new) for x in st]
    lt = functools.reduce(jnp.add, pt)
    l_ref[h] = alpha * l_ref[h] + jnp.broadcast_to(jnp.sum(lt, axis=1, keepdims=True), (rows, LANES))
    p = pt[0] if len(pt) == 1 else jnp.concatenate(pt, axis=1)
    acc_ref[h] = alpha * acc_ref[h] + jnp.dot(p.astype(BF16), v, preferred_element_type=F32)
    m_ref[h] = m_new


def _flash_init(m_ref, l_ref, acc_ref):
    m_ref[...] = jnp.full(m_ref.shape, NEG_BIG, F32)
    l_ref[...] = jnp.zeros(l_ref.shape, F32)
    acc_ref[...] = jnp.zeros(acc_ref.shape, F32)


def _topk_cols(xt_ref, sel_ref, k):
    nb, c = xt_ref.shape
    slabs = [(r0, xt_ref[r0:r0 + 8, :]) for r0 in range(0, nb, 8)]
    row = lax.broadcasted_iota(I32, (8, c), 0)

    def count(pred):
        acc = jnp.zeros((8, c), I32)
        for r0, blk in slabs:
            acc = acc + jnp.where(pred(blk, r0), 1, 0)
        return jnp.broadcast_to(jnp.sum(acc, axis=0, keepdims=True), (8, c))

    def bit_body(it, t):
        cand = t + (jnp.int32(1) << (31 - it))
        cand_f = _key_to_float(cand)
        return jnp.where(count(lambda blk, r0: blk >= cand_f) >= k, cand, t)

    zero = jnp.zeros((8, c), I32)
    t = lax.fori_loop(0, 32, bit_body, jnp.full((8, c), -2 ** 31, I32))
    thr = _key_to_float(jnp.maximum(t, KEY_NEG_INF))
    short = thr == -jnp.inf
    n_gt = count(lambda blk, r0: blk > thr)
    excess = jnp.where(short, 0, count(lambda blk, r0: blk >= thr) - k)
    need = k - n_gt
    n_bits = max(1, int(nb).bit_length())

    def tie_limit():
        jl = zero
        for it in range(n_bits):
            cand = jl + (1 << (n_bits - 1 - it))
            cnt = count(lambda blk, r0: (blk == thr) & ((row + r0) < cand))
            jl = jnp.where(cnt <= need, cand, jl)
        return jl

    jlim = lax.cond(jnp.max(excess) > 0, tie_limit, lambda: jnp.full((8, c), 2 ** 30, I32))
    jlim = jnp.where(short, 0, jlim)
    for r0, blk in slabs:
        sel_ref[r0:r0 + 8, :] = jnp.where((blk > thr) | ((blk == thr) & ((row + r0) < jlim)), 1.0, 0.0)


def _nsa_prompt_kernel(q_ref, gate_ref, p0_ref, p1_ref, ov_ref, ks_ref, vs_ref, kw_ref, vw_ref, o_ref,
                       kc_ref, vc_ref, impt_ref, selt_ref, oc_ref, os_ref, m_ref, l_ref, acc_ref,
                       *, tq, tk, wk, n_cmp, n_sel):
    i = pl.program_id(1)
    q0 = i * tq
    nc = p0_ref.shape[0]

    @pl.when(i == 0)
    def _():
        kc = p0_ref[...] + pltpu.roll(p1_ref[...], nc - 1, 0)
        for g in range(B_KV_HEADS):
            kc_ref[g] = kc[:, (2 * g) * LANES:(2 * g + 1) * LANES].astype(BF16)
            vc_ref[g] = kc[:, (2 * g + 1) * LANES:(2 * g + 2) * LANES].astype(BF16)

    tpos_c = q0 + lax.broadcasted_iota(I32, (tq, nc), 0)
    nidx = lax.broadcasted_iota(I32, (tq, nc), 1)
    cvalid = jnp.where(nidx < n_cmp, nidx * CMP_STRIDE + (CMP_BLOCK - 1), 2 ** 30) <= tpos_c
    tpos = q0 + lax.broadcasted_iota(I32, (tq, LANES), 0)
    jblk = lax.broadcasted_iota(I32, (tq, LANES), 1)
    cur = tpos >> 6
    forced = (jblk == 0) | (jblk == cur) | (jblk == cur - 1)
    admissible = jblk * SLC_BLOCK <= tpos
    sel = []
    for g in range(B_KV_HEADS):
        pg = jnp.zeros((tq, nc), F32)
        for r in range(GB):
            h = g * GB + r
            s = lax.dot_general(q_ref[:, h * LANES:(h + 1) * LANES], kc_ref[g], _NT,
                                preferred_element_type=F32)
            s = jnp.where(cvalid, s, NEG_BIG)
            e = jnp.where(cvalid, jnp.exp2(s - jnp.max(s, axis=1, keepdims=True)), 0.0)
            l = jnp.sum(e, axis=1, keepdims=True)
            p = e / jnp.where(l > 0.0, l, 1.0)
            oc_ref[h] = jnp.dot(p.astype(BF16), vc_ref[g], preferred_element_type=F32)
            pg = pg + p
        imp = _dot_f32(pg, ov_ref[...])
        imp = jnp.where(forced, jnp.inf, jnp.where(admissible, imp, -jnp.inf))
        impt_ref[...] = imp.T
        _topk_cols(impt_ref, selt_ref, n_sel)
        sel.append(selt_ref[...].T.astype(BF16))

    nkb = lax.div(q0 + tq + tk - 1, tk)
    _flash_init(m_ref, l_ref, acc_ref)
    blk_row = lax.broadcasted_iota(I32, (LANES, tk), 0)
    key_col = lax.broadcasted_iota(I32, (LANES, tk), 1)
    tpos_k = q0 + lax.broadcasted_iota(I32, (tq, tk), 0)
    kcol = lax.broadcasted_iota(I32, (tq, tk), 1)

    def slc_body(j, carry):
        k0 = pl.multiple_of(j * tk, tk)
        expand = jnp.where(((key_col + k0) >> 6) == blk_row, 1.0, 0.0).astype(BF16)
        causal = (kcol + k0) <= tpos_k
        for g in range(B_KV_HEADS):
            hit = jnp.dot(sel[g], expand, preferred_element_type=F32)
            bias = _mask_bias(jnp.where(causal, hit, 0.0) > 0.5)
            bias = [bias[:, t * LANES:(t + 1) * LANES] for t in range(tk // LANES)]
            for r in range(GB):
                h = g * GB + r
                _flash_step(q_ref[:, h * LANES:(h + 1) * LANES], ks_ref[g, pl.ds(k0, tk), :],
                            vs_ref[g, pl.ds(k0, tk), :], bias, m_ref, l_ref, acc_ref, h)
        return carry

    lax.fori_loop(0, nkb, slc_body, 0)
    for h in range(N_HEADS):
        os_ref[h] = acc_ref[h] / l_ref[h]

    w0 = pl.multiple_of(jnp.maximum(q0 - WINDOW, 0), tq)
    kpos = w0 + lax.broadcasted_iota(I32, (tq, wk), 1)
    tpos_w = q0 + lax.broadcasted_iota(I32, (tq, wk), 0)
    bias_w = _mask_bias(jnp.where(kpos <= tpos_w, kpos, -2 ** 30) > tpos_w - WINDOW)
    for h in range(N_HEADS):
        g = h // GB
        s = lax.dot_general(q_ref[:, h * LANES:(h + 1) * LANES], kw_ref[g, pl.ds(w0, wk), :], _NT,
                            preferred_element_type=F32) + bias_w
        p = jnp.exp2(s - jnp.max(s, axis=1, keepdims=True))
        ow = (jnp.dot(p.astype(BF16), vw_ref[g, pl.ds(w0, wk), :], preferred_element_type=F32)
              / jnp.sum(p, axis=1, keepdims=True))
        o_ref[:, h * LANES:(h + 1) * LANES] = (gate_ref[:, 3 * h:3 * h + 1] * oc_ref[h]
                                               + gate_ref[:, 3 * h + 1:3 * h + 2] * os_ref[h]
                                               + gate_ref[:, 3 * h + 2:3 * h + 3] * ow)


def _nsa_prompt(q, gates, p0, p1, ks, vs, kw, vw, batch, seq, tq=512, tk=512):
    tq = min(tq, seq)
    tk = min(tk, seq)
    nq = seq // tq
    n = batch * seq
    nc = seq // CMP_STRIDE
    n_cmp = nc - CMP_BLOCK // CMP_STRIDE + 1
    n_slc = seq // SLC_BLOCK
    assert n_slc <= LANES and WINDOW % tq == 0
    ov = _overlap_matrix(nc, n_cmp, LANES)
    row = lambda b, i: (b * nq + i, 0)
    per_b = lambda b, i: (b, 0)
    hm_b = lambda b, i: (0, b, 0)
    once = pl.Buffered(1)
    kern = functools.partial(_nsa_prompt_kernel, tq=tq, tk=tk, wk=min(WINDOW + tq, seq), n_cmp=n_cmp,
                             n_sel=min(SLC_TOPN, n_slc))
    hm_spec = pl.BlockSpec((B_KV_HEADS, seq, HEAD_DIM), hm_b, pipeline_mode=once)
    return pl.pallas_call(
        kern,
        grid=(batch, nq),
        in_specs=[
            pl.BlockSpec((tq, WIDTH), row),
            pl.BlockSpec((tq, LANES), row),
            pl.BlockSpec((nc, B_KVW), per_b, pipeline_mode=once),
            pl.BlockSpec((nc, B_KVW), per_b, pipeline_mode=once),
            pl.BlockSpec((nc, LANES), lambda b, i: (0, 0), pipeline_mode=once),
            hm_spec, hm_spec, hm_spec, hm_spec,
        ],
        out_specs=pl.BlockSpec((tq, WIDTH), row),
        out_shape=jax.ShapeDtypeStruct((n, WIDTH), F32),
        scratch_shapes=[
            pltpu.VMEM((B_KV_HEADS, nc, HEAD_DIM), BF16),
            pltpu.VMEM((B_KV_HEADS, nc, HEAD_DIM), BF16),
            pltpu.VMEM((LANES, tq), F32),
            pltpu.VMEM((LANES, tq), F32),
            pltpu.VMEM((N_HEADS, tq, HEAD_DIM), F32),
            pltpu.VMEM((N_HEADS, tq, HEAD_DIM), F32),
            pltpu.VMEM((N_HEADS, tq, LANES), F32),
            pltpu.VMEM((N_HEADS, tq, LANES), F32),
            pltpu.VMEM((N_HEADS, tq, HEAD_DIM), F32),
        ],
        compiler_params=_cparams(2),
        name="nsa_prompt",
    )(q, gates, p0, p1, ov, ks, vs, kw, vw)


def _nsa_prompt_layer(x, norm_g, w_in, q_g, k_g, w_cmp, pe_cmp, w_o):
    b, s, _ = x.shape
    n = b * s
    tm = min(256, s)
    xf = x.reshape(n, D_MODEL)
    q, kvc, kvs, kvw, ks, vs, kw, vw, gates, z = _proj_b(
        xf, norm_g, _layout_w_b(w_in), q_g, k_g, _rope_tables(jnp.arange(s), ROT_DIM), tm)
    wc, pe = _layout_cmp(w_cmp, pe_cmp)
    nc = s // CMP_STRIDE
    p0, p1 = _cmp_mm(kvc.reshape(b * nc, CHUNK_COLS), wc, pe, min(128, nc))
    o = _nsa_prompt(q, gates, p0, p1, ks, vs, kw, vw, b, s)
    y = _out_proj(xf, o, z, w_o, tm)
    shp = (b, s, B_KV_HEADS, 2, HEAD_DIM)
    wlen = min(WINDOW, s)
    return (y.reshape(b, s, D_MODEL), kvc.reshape(shp), kvs.reshape(shp),
            kvw.reshape(shp)[:, s - wlen:])


def _page_specs(block, pages_per_step):
    return [pl.BlockSpec((None,) + block,
                         lambda b, j, pt, k=k: (pt[b, j * pages_per_step + k],) + (0,) * len(block))
            for k in range(pages_per_step)]


def _split_tiles(n_tiles, max_group=16):
    group = max(g for g in range(1, max_group + 1) if n_tiles % g == 0)
    return n_tiles // group, group


def _row_token(shape):
    return lax.broadcasted_iota(I32, shape, 0) & 3


def _dsa_sample_score_kernel(pt_ref, iq_ref, iw_ref, ikn_ref, *rest, pps, past_len, n_keep):
    pages = rest[:pps]
    sc_ref, thr_ref, jlim_ref = rest[pps:]
    j = pl.program_id(1)
    iq = iq_ref[:, :IDX_DIM]

    def score(keys_bf16):
        lg = lax.dot_general(iq, keys_bf16, _NT, preferred_element_type=F32)
        acc = jnp.zeros((8, LANES), F32)
        for h in range(IDX_HEADS):
            acc = acc + iw_ref[h * 8:(h + 1) * 8, :] * jnp.maximum(lg[h * 8:(h + 1) * 8, :], 0.0)
        return acc

    for k in range(pps):
        c0 = pl.multiple_of((j * pps + k) * PAGE_SIZE, PAGE_SIZE)
        sc_ref[:, pl.ds(c0, PAGE_SIZE)] = score(pages[k][...].astype(BF16))

    @pl.when(j == pl.num_programs(1) - 1)
    def _():
        s_new = score(ikn_ref[:, :IDX_DIM])
        col = lax.broadcasted_iota(I32, (8, LANES), 1)
        sc_ref[:, past_len:past_len + LANES] = jnp.where(col <= _row_token((8, LANES)), s_new, -jnp.inf)
        n_groups, group = _split_tiles(sc_ref.shape[1] // LANES)
        thr, jpos, jneg = _topk_threshold(sc_ref, n_groups, group, n_keep)
        thr_ref[...] = thr
        jlim_ref[0] = jpos
        jlim_ref[1] = jneg


def _dsa_sample_scores(page_table, cache_idx, iq_s, iw_s, ik_new, n_keep, pps=8):
    nb, n_pages = page_table.shape
    past_len = n_pages * PAGE_SIZE
    scw = past_len + LANES
    per_b = lambda b, j, pt: (b, 0, 0)
    kern = functools.partial(_dsa_sample_score_kernel, pps=pps, past_len=past_len, n_keep=n_keep)
    grid_spec = pltpu.PrefetchScalarGridSpec(
        num_scalar_prefetch=1,
        grid=(nb, n_pages // pps),
        in_specs=[pl.BlockSpec((None, 64, LANES), per_b),
                  pl.BlockSpec((None, 64, LANES), per_b),
                  pl.BlockSpec((None, LANES, LANES), per_b)] + _page_specs((PAGE_SIZE, IDX_DIM), pps),
        out_specs=[pl.BlockSpec((None, 8, scw), per_b),
                   pl.BlockSpec((None, 8, LANES), per_b),
                   pl.BlockSpec((None, 2, 8, LANES), lambda b, j, pt: (b, 0, 0, 0))],
    )
    return pl.pallas_call(
        kern,
        grid_spec=grid_spec,
        out_shape=[jax.ShapeDtypeStruct((nb, 8, scw), F32),
                   jax.ShapeDtypeStruct((nb, 8, LANES), F32),
                   jax.ShapeDtypeStruct((nb, 2, 8, LANES), I32)],
        compiler_params=_cparams(2),
        name="dsa_sample_scores",
    )(page_table, iq_s, iw_s, ik_new, *([cache_idx] * pps))


def _dsa_sample_attn_kernel(pt_ref, q_ref, sc_ref, thr_ref, jlim_ref, kn_ref, vn_ref, *rest, pps, past_len):
    pages = rest[:pps]
    o_ref, kbuf_ref, vbuf_ref, m_ref, l_ref, acc_ref = rest[pps:]
    j = pl.program_id(1)
    thr = thr_ref[...]
    jpos = jlim_ref[0]
    jneg = jlim_ref[1]
    lane = lax.broadcasted_iota(I32, (8, LANES), 1)
    rows_per_token = 2 * A_KV_HEADS

    @pl.when(j == 0)
    def _():
        _flash_init(m_ref, l_ref, acc_ref)

    def bias_at(c0):
        return _mask_bias(_selected(sc_ref[:, pl.ds(c0, LANES)], thr, jpos, jneg, lane + c0))

    for k in range(pps):
        for g in range(A_KV_HEADS):
            rows = pl.ds(k * PAGE_SIZE, PAGE_SIZE)
            kbuf_ref[g, rows, :] = pages[k][pl.ds(2 * g, PAGE_SIZE, stride=rows_per_token), :].astype(BF16)
            vbuf_ref[g, rows, :] = pages[k][pl.ds(2 * g + 1, PAGE_SIZE, stride=rows_per_token), :].astype(BF16)
    bias = [bias_at(pl.multiple_of((j * pps + k) * PAGE_SIZE, PAGE_SIZE)) for k in range(pps)]
    for g in range(A_KV_HEADS):
        _flash_step(q_ref[g], kbuf_ref[g], vbuf_ref[g], bias, m_ref, l_ref, acc_ref, g)

    @pl.when(j == pl.num_programs(1) - 1)
    def _():
        bias_new = [bias_at(past_len)]
        for g in range(A_KV_HEADS):
            _flash_step(q_ref[g], kn_ref[g], vn_ref[g], bias_new, m_ref, l_ref, acc_ref, g)
            o_ref[g] = acc_ref[g] / l_ref[g]


def _dsa_sample_attn(page_table, cache_kv, q_s, scores, thr, jlim, k_new, v_new, pps=8):
    nb, n_pages = page_table.shape
    pps = min(pps, n_pages)
    past_len = n_pages * PAGE_SIZE
    scw = scores.shape[-1]
    per_b3 = lambda b, j, pt: (b, 0, 0)
    per_b4 = lambda b, j, pt: (b, 0, 0, 0)
    page_rows = PAGE_SIZE * 2 * A_KV_HEADS
    kern = functools.partial(_dsa_sample_attn_kernel, pps=pps, past_len=past_len)
    grid_spec = pltpu.PrefetchScalarGridSpec(
        num_scalar_prefetch=1,
        grid=(nb, n_pages // pps),
        in_specs=[pl.BlockSpec((None, A_KV_HEADS, 8, HEAD_DIM), per_b4),
                  pl.BlockSpec((None, 8, scw), per_b3),
                  pl.BlockSpec((None, 8, LANES), per_b3),
                  pl.BlockSpec((None, 2, 8, LANES), per_b4),
                  pl.BlockSpec((None, A_KV_HEADS, LANES, HEAD_DIM), per_b4),
                  pl.BlockSpec((None, A_KV_HEADS, LANES, HEAD_DIM), per_b4)]
        + _page_specs((page_rows, HEAD_DIM), pps),
        out_specs=pl.BlockSpec((None, A_KV_HEADS, 8, HEAD_DIM), per_b4),
        scratch_shapes=[pltpu.VMEM((A_KV_HEADS, pps * PAGE_SIZE, HEAD_DIM), BF16),
                        pltpu.VMEM((A_KV_HEADS, pps * PAGE_SIZE, HEAD_DIM), BF16),
                        pltpu.VMEM((A_KV_HEADS, 8, LANES), F32),
                        pltpu.VMEM((A_KV_HEADS, 8, LANES), F32),
                        pltpu.VMEM((A_KV_HEADS, 8, HEAD_DIM), F32)],
    )
    return pl.pallas_call(
        kern,
        grid_spec=grid_spec,
        out_shape=jax.ShapeDtypeStruct((nb, A_KV_HEADS, 8, HEAD_DIM), F32),
        compiler_params=_cparams(2),
        name="dsa_sample_attn",
    )(page_table, q_s, scores, thr, jlim, k_new, v_new,
      *([cache_kv.reshape(cache_kv.shape[0], page_rows, HEAD_DIM)] * pps))


def _pad_keys(a, axis):
    pad = [(0, 0)] * a.ndim
    pad[axis] = (0, LANES - a.shape[axis])
    return jnp.pad(a, pad)


def _dsa_sample_layer(x, cache_kv, cache_idx, page_table, norm_g, w_in, q_g, k_g, w_o):
    nb, nt, _ = x.shape
    assert nt == 4
    n = nb * nt
    past_len = page_table.shape[1] * PAGE_SIZE
    pos = jnp.tile(past_len + jnp.arange(nt), nb)
    xf = x.reshape(n, D_MODEL)
    q, kv, k_hm, v_hm, iq, ikw, ikp, z = _proj_a(
        xf, norm_g, _layout_w_a(w_in), q_g, k_g, _rope_tables(pos, ROT_DIM), _rope_tables(pos, IDX_ROT), n)
    ga = N_HEADS // A_KV_HEADS
    iq_s = iq.reshape(nb, nt, IDX_HEADS, LANES).transpose(0, 2, 1, 3)
    iq_s = jnp.concatenate([iq_s, iq_s], axis=2).reshape(nb, 64, LANES)
    iw = ikw[:, IDX_DIM:IDX_DIM + IDX_HEADS].reshape(nb, nt, IDX_HEADS).transpose(0, 2, 1)
    iw_s = jnp.broadcast_to(jnp.concatenate([iw, iw], axis=2).reshape(nb, 64, 1), (nb, 64, LANES))
    ik_new = _pad_keys(ikp.reshape(nb, nt, LANES), 1)
    q_s = q.reshape(nb, nt, A_KV_HEADS, ga, HEAD_DIM).transpose(0, 2, 3, 1, 4).reshape(nb, A_KV_HEADS, 8, HEAD_DIM)
    k_new = _pad_keys(k_hm.reshape(A_KV_HEADS, nb, nt, HEAD_DIM).transpose(1, 0, 2, 3), 2)
    v_new = _pad_keys(v_hm.reshape(A_KV_HEADS, nb, nt, HEAD_DIM).transpose(1, 0, 2, 3), 2)
    n_keep = min(TOPK_MAX, (past_len + nt) // 4)
    scores, thr, jlim = _dsa_sample_scores(page_table, cache_idx, iq_s, iw_s, ik_new, n_keep)
    o_s = _dsa_sample_attn(page_table, cache_kv, q_s, scores, thr, jlim, k_new, v_new)
    o = o_s.reshape(nb, A_KV_HEADS, ga, nt, HEAD_DIM).transpose(0, 3, 1, 2, 4).reshape(n, WIDTH)
    y = _out_proj(xf, o, z, w_o, n)
    return (y.reshape(nb, nt, D_MODEL), kv.reshape(nb, nt, A_KV_HEADS, 2, HEAD_DIM),
            ikw[:, :IDX_DIM].reshape(nb, nt, IDX_DIM))


CHUNKS_PER_PAGE = PAGE_SIZE // CMP_STRIDE


SLABS_B = 2 * B_KV_HEADS
SLAB_COLS = CMP_STRIDE * HEAD_DIM


def _layout_cmp_slabs(w_cmp, pe_cmp):
    r = CMP_BLOCK // CMP_STRIDE
    w = w_cmp.reshape(r, CMP_STRIDE, 2, HEAD_DIM, HEAD_DIM).transpose(0, 2, 1, 3, 4)
    pe = pe_cmp.reshape(r, CMP_STRIDE, 2, HEAD_DIM).transpose(0, 2, 1, 3)
    return w.reshape(r, 2, SLAB_COLS, HEAD_DIM).astype(BF16), pe.reshape(r, 2, 1, SLAB_COLS)


def _cmp_paged_kernel(pt_ref, xn_ref, pe_ref, w_ref, *rest, pps):
    pages = rest[:pps]
    p0_ref, p1_ref, pn_ref, xs_ref = rest[pps:]
    for k in range(pps):
        for l in range(CMP_STRIDE):
            for s in range(SLABS_B):
                xs_ref[s, k * CHUNKS_PER_PAGE:(k + 1) * CHUNKS_PER_PAGE, l * HEAD_DIM:(l + 1) * HEAD_DIM] = (
                    pages[k][pl.ds(l * SLABS_B + s, CHUNKS_PER_PAGE, stride=CMP_STRIDE * SLABS_B), :])
    for s in range(SLABS_B):
        c = s % 2
        x = xs_ref[s]
        cols = slice(s * HEAD_DIM, (s + 1) * HEAD_DIM)
        p0_ref[:, cols] = jnp.dot((x + pe_ref[0, c]).astype(BF16), w_ref[0, c], preferred_element_type=F32)
        p1_ref[:, cols] = jnp.dot((x + pe_ref[1, c]).astype(BF16), w_ref[1, c], preferred_element_type=F32)
        pn_ref[:, cols] = jnp.dot((xn_ref[s] + pe_ref[1, c]).astype(BF16), w_ref[1, c],
                                  preferred_element_type=F32)


def _cmp_paged(page_table, cache_cmp, x_new, w, pe, pps=32):
    nb, n_pages = page_table.shape
    pps = min(pps, n_pages)
    steps = n_pages // pps
    rows = pps * CHUNKS_PER_PAGE
    kern = functools.partial(_cmp_paged_kernel, pps=pps)
    grid_spec = pltpu.PrefetchScalarGridSpec(
        num_scalar_prefetch=1,
        grid=(nb, steps),
        in_specs=[pl.BlockSpec((None, SLABS_B, 8, SLAB_COLS), lambda b, j, pt: (b, 0, 0, 0)),
                  pl.BlockSpec((2, 2, 1, SLAB_COLS), lambda b, j, pt: (0, 0, 0, 0)),
                  pl.BlockSpec((2, 2, SLAB_COLS, HEAD_DIM), lambda b, j, pt: (0, 0, 0, 0))]
        + _page_specs((PAGE_SIZE * SLABS_B, HEAD_DIM), pps),
        out_specs=[pl.BlockSpec((rows, B_KVW), lambda b, j, pt: (b * steps + j, 0)),
                   pl.BlockSpec((rows, B_KVW), lambda b, j, pt: (b * steps + j, 0)),
                   pl.BlockSpec((None, 8, B_KVW), lambda b, j, pt: (b, 0, 0))],
        scratch_shapes=[pltpu.VMEM((SLABS_B, rows, SLAB_COLS), F32)],
    )
    n_chunks = n_pages * CHUNKS_PER_PAGE
    return pl.pallas_call(
        kern,
        grid_spec=grid_spec,
        out_shape=[jax.ShapeDtypeStruct((nb * n_chunks, B_KVW), F32),
                   jax.ShapeDtypeStruct((nb * n_chunks, B_KVW), F32),
                   jax.ShapeDtypeStruct((nb, 8, B_KVW), F32)],
        compiler_params=_cparams(2),
        name="cmp_paged",
    )(page_table, x_new, pe, w,
      *([cache_cmp.reshape(cache_cmp.shape[0], PAGE_SIZE * SLABS_B, HEAD_DIM)] * pps))


QROWS_B = GB * 4
EXPAND_COLS = 1024


def _nsa_sample_kernel(pt_ref, q_ref, gate_ref, p0_ref, p1_ref, pn_ref, ov_ref, ksn_ref, vsn_ref,
                       win_ref, kwn_ref, vwn_ref, *rest, pps, past_len, n_cmp, n_sel):
    pages = rest[:pps]
    (o_ref, kc_ref, vc_ref, imp_ref, bias_ref, oc_ref, ow_ref, kbuf_ref, vbuf_ref,
     m_ref, l_ref, acc_ref) = rest[pps:]
    j = pl.program_id(1)
    nc = p0_ref.shape[0]
    tok = _row_token((QROWS_B, LANES))
    lane = lax.broadcasted_iota(I32, (QROWS_B, LANES), 1)
    rows_per_token = 2 * B_KV_HEADS

    @pl.when(j == 0)
    def _():
        rown = lax.broadcasted_iota(I32, (nc, B_KVW), 0)
        kc = p0_ref[...] + jnp.where(rown == nc - 1, jnp.broadcast_to(pn_ref[0:1, :], (nc, B_KVW)),
                                     pltpu.roll(p1_ref[...], nc - 1, 0))
        for g in range(B_KV_HEADS):
            kc_ref[g] = kc[:, (2 * g) * LANES:(2 * g + 1) * LANES].astype(BF16)
            vc_ref[g] = kc[:, (2 * g + 1) * LANES:(2 * g + 2) * LANES].astype(BF16)
        nidx = lax.broadcasted_iota(I32, (QROWS_B, nc), 1)
        tpos_c = past_len + _row_token((QROWS_B, nc))
        cvalid = jnp.where(nidx < n_cmp, nidx * CMP_STRIDE + (CMP_BLOCK - 1), 2 ** 30) <= tpos_c
        n_lt = imp_ref.shape[1] // LANES
        imp_shape = (QROWS_B, imp_ref.shape[1])
        jblk = lax.broadcasted_iota(I32, imp_shape, 1)
        tpos_i = past_len + _row_token(imp_shape)
        cur = tpos_i >> 6
        forced = (jblk == 0) | (jblk == cur) | (jblk == cur - 1)
        admissible = jblk * SLC_BLOCK <= tpos_i
        wb = win_ref.shape[0] // rows_per_token
        widx = lax.broadcasted_iota(I32, (QROWS_B, wb), 1)
        bias_w = jnp.concatenate([_mask_bias((widx - wb) > (_row_token((QROWS_B, wb)) - WINDOW)),
                                  _mask_bias(lane <= tok)], axis=1)
        for g in range(B_KV_HEADS):
            q = q_ref[g]
            s = lax.dot_general(q, kc_ref[g], _NT, preferred_element_type=F32)
            s = jnp.where(cvalid, s, NEG_BIG)
            e = jnp.where(cvalid, jnp.exp2(s - jnp.max(s, axis=1, keepdims=True)), 0.0)
            l = jnp.sum(e, axis=1, keepdims=True)
            p = e / jnp.where(l > 0.0, l, 1.0)
            oc_ref[g] = jnp.dot(p.astype(BF16), vc_ref[g], preferred_element_type=F32)
            pg = p[0:8] + p[8:16]
            pg = pg + pltpu.roll(pg, 4, 0)
            imp = _dot_f32(jnp.concatenate([pg, pg], axis=0), ov_ref[...])
            imp_ref[g * QROWS_B:(g + 1) * QROWS_B, :] = jnp.where(
                forced, jnp.inf, jnp.where(admissible, imp, -jnp.inf))

            kw = jnp.concatenate([win_ref[pl.ds(2 * g, wb, stride=rows_per_token), :].astype(BF16), kwn_ref[g]],
                                 axis=0)
            vw = jnp.concatenate([win_ref[pl.ds(2 * g + 1, wb, stride=rows_per_token), :].astype(BF16),
                                  vwn_ref[g]], axis=0)
            sw = lax.dot_general(q, kw, _NT, preferred_element_type=F32) + bias_w
            pw = jnp.exp2(sw - jnp.max(sw, axis=1, keepdims=True))
            ow_ref[g] = (jnp.dot(pw.astype(BF16), vw, preferred_element_type=F32)
                         / jnp.sum(pw, axis=1, keepdims=True))

        n_rows = B_KV_HEADS * QROWS_B
        thr, jpos, jneg = _topk_threshold(imp_ref, 1, n_lt, n_sel)
        lane_a = lax.broadcasted_iota(I32, (n_rows, LANES), 1)
        selb = jnp.concatenate(
            [jnp.where(_selected(imp_ref[:, c * LANES:(c + 1) * LANES], thr, jpos, jneg, lane_a + c * LANES),
                       1.0, 0.0) for c in range(n_lt)], axis=1).astype(BF16)
        total = bias_ref.shape[2]
        c0 = 0
        while c0 < total:
            w = min(EXPAND_COLS, total - c0)
            blk_of_key = (lax.broadcasted_iota(I32, (imp_ref.shape[1], w), 1) + c0) >> 6
            ex = jnp.where(blk_of_key == lax.broadcasted_iota(I32, (imp_ref.shape[1], w), 0), 1.0, 0.0)
            hit = jnp.dot(selb, ex.astype(BF16), preferred_element_type=F32)
            kpos = lax.broadcasted_iota(I32, (n_rows, w), 1) + c0
            visible = jnp.where(kpos <= past_len + _row_token((n_rows, w)), hit, 0.0) > 0.5
            bias = _mask_bias(visible)
            for g in range(B_KV_HEADS):
                bias_ref[g, :, c0:c0 + w] = bias[g * QROWS_B:(g + 1) * QROWS_B]
            c0 += w
        _flash_init(m_ref, l_ref, acc_ref)

    for k in range(pps):
        for g in range(B_KV_HEADS):
            rows = pl.ds(k * PAGE_SIZE, PAGE_SIZE)
            kbuf_ref[g, rows, :] = pages[k][pl.ds(2 * g, PAGE_SIZE, stride=rows_per_token), :].astype(BF16)
            vbuf_ref[g, rows, :] = pages[k][pl.ds(2 * g + 1, PAGE_SIZE, stride=rows_per_token), :].astype(BF16)
    for g in range(B_KV_HEADS):
        bias = [bias_ref[g, :, pl.ds(pl.multiple_of((j * pps + k) * PAGE_SIZE, PAGE_SIZE), PAGE_SIZE)]
                for k in range(pps)]
        _flash_step(q_ref[g], kbuf_ref[g], vbuf_ref[g], bias, m_ref, l_ref, acc_ref, g)

    @pl.when(j == pl.num_programs(1) - 1)
    def _():
        for g in range(B_KV_HEADS):
            _flash_step(q_ref[g], ksn_ref[g], vsn_ref[g], [bias_ref[g, :, past_len:past_len + LANES]],
                        m_ref, l_ref, acc_ref, g)
            o_ref[g] = (gate_ref[g, 0] * oc_ref[g] + gate_ref[g, 1] * (acc_ref[g] / l_ref[g])
                        + gate_ref[g, 2] * ow_ref[g])


def _nsa_sample(page_table, cache_slc, state_win, q_s, gates_s, p0, p1, pn, ks_new, vs_new, kw_new, vw_new,
                pps=16):
    nb, n_pages = page_table.shape
    pps = min(pps, n_pages)
    past_len = n_pages * PAGE_SIZE
    total = past_len + 4
    ncz = -(-total // CMP_STRIDE)
    nc = ncz - 1
    n_cmp = ncz - CMP_BLOCK // CMP_STRIDE + 1
    assert nc == n_pages * CHUNKS_PER_PAGE and n_cmp == nc
    n_slc = -(-total // SLC_BLOCK)
    imp_w = -(-n_slc // LANES) * LANES
    ov = _overlap_matrix(nc, n_cmp, imp_w)
    wb = state_win.shape[1]
    slabs = 2 * B_KV_HEADS
    per_b3 = lambda b, j, pt: (b, 0, 0)
    per_b4 = lambda b, j, pt: (b, 0, 0, 0)
    once = pl.Buffered(1)
    kern = functools.partial(_nsa_sample_kernel, pps=pps, past_len=past_len, n_cmp=n_cmp,
                             n_sel=min(SLC_TOPN, n_slc))
    new_spec = pl.BlockSpec((None, B_KV_HEADS, LANES, HEAD_DIM), per_b4)
    grid_spec = pltpu.PrefetchScalarGridSpec(
        num_scalar_prefetch=1,
        grid=(nb, n_pages // pps),
        in_specs=[pl.BlockSpec((None, B_KV_HEADS, QROWS_B, HEAD_DIM), per_b4),
                  pl.BlockSpec((None, B_KV_HEADS, 3, QROWS_B, HEAD_DIM), lambda b, j, pt: (b, 0, 0, 0, 0)),
                  pl.BlockSpec((nc, B_KVW), lambda b, j, pt: (b, 0)),
                  pl.BlockSpec((nc, B_KVW), lambda b, j, pt: (b, 0)),
                  pl.BlockSpec((None, 8, B_KVW), per_b3),
                  pl.BlockSpec((nc, imp_w), lambda b, j, pt: (0, 0), pipeline_mode=once),
                  new_spec, new_spec,
                  pl.BlockSpec((None, wb * slabs, HEAD_DIM), per_b3),
                  new_spec, new_spec]
        + _page_specs((PAGE_SIZE * slabs, HEAD_DIM), pps),
        out_specs=pl.BlockSpec((None, B_KV_HEADS, QROWS_B, HEAD_DIM), per_b4),
        scratch_shapes=[pltpu.VMEM((B_KV_HEADS, nc, HEAD_DIM), BF16),
                        pltpu.VMEM((B_KV_HEADS, nc, HEAD_DIM), BF16),
                        pltpu.VMEM((B_KV_HEADS * QROWS_B, imp_w), F32),
                        pltpu.VMEM((B_KV_HEADS, QROWS_B, past_len + LANES), F32),
                        pltpu.VMEM((B_KV_HEADS, QROWS_B, HEAD_DIM), F32),
                        pltpu.VMEM((B_KV_HEADS, QROWS_B, HEAD_DIM), F32),
                        pltpu.VMEM((B_KV_HEADS, pps * PAGE_SIZE, HEAD_DIM), BF16),
                        pltpu.VMEM((B_KV_HEADS, pps * PAGE_SIZE, HEAD_DIM), BF16),
                        pltpu.VMEM((B_KV_HEADS, QROWS_B, LANES), F32),
                        pltpu.VMEM((B_KV_HEADS, QROWS_B, LANES), F32),
                        pltpu.VMEM((B_KV_HEADS, QROWS_B, HEAD_DIM), F32)],
    )
    return pl.pallas_call(
        kern,
        grid_spec=grid_spec,
        out_shape=jax.ShapeDtypeStruct((nb, B_KV_HEADS, QROWS_B, HEAD_DIM), F32),
        compiler_params=_cparams(2),
        name="nsa_sample",
    )(page_table, q_s, gates_s, p0, p1, pn, ov, ks_new, vs_new,
      state_win.reshape(nb, wb * slabs, HEAD_DIM), kw_new, vw_new,
      *([cache_slc.reshape(cache_slc.shape[0], PAGE_SIZE * slabs, HEAD_DIM)] * pps))


def _nsa_sample_layer(x, cache_cmp, cache_slc, state_win, page_table, norm_g, w_in, q_g, k_g,
                      w_cmp, pe_cmp, w_o):
    nb, nt, _ = x.shape
    assert nt == 4
    n = nb * nt
    past_len = page_table.shape[1] * PAGE_SIZE
    pos = jnp.tile(past_len + jnp.arange(nt), nb)
    xf = x.reshape(n, D_MODEL)
    q, kvc, kvs, kvw, ks, vs, kw, vw, gates, z = _proj_b(
        xf, norm_g, _layout_w_b(w_in), q_g, k_g, _rope_tables(pos, ROT_DIM), n)
    wc, pe = _layout_cmp_slabs(w_cmp, pe_cmp)
    x_new = kvc.reshape(nb, nt, SLABS_B, HEAD_DIM).transpose(0, 2, 1, 3).reshape(nb, SLABS_B, 1, nt * HEAD_DIM)
    x_new = jnp.pad(x_new, ((0, 0), (0, 0), (0, 7), (0, SLAB_COLS - nt * HEAD_DIM)))
    p0, p1, pn = _cmp_paged(page_table, cache_cmp, x_new, wc, pe)
    q_s = q.reshape(nb, nt, B_KV_HEADS, GB, HEAD_DIM).transpose(0, 2, 3, 1, 4).reshape(nb, B_KV_HEADS, QROWS_B, HEAD_DIM)
    gt = gates[:, :3 * N_HEADS].reshape(nb, nt, B_KV_HEADS, GB, 3).transpose(0, 2, 4, 3, 1)
    gates_s = jnp.broadcast_to(gt.reshape(nb, B_KV_HEADS, 3, QROWS_B, 1), (nb, B_KV_HEADS, 3, QROWS_B, HEAD_DIM))
    new_keys = lambda a: _pad_keys(a.reshape(B_KV_HEADS, nb, nt, HEAD_DIM).transpose(1, 0, 2, 3), 2)
    o_s = _nsa_sample(page_table, cache_slc, state_win, q_s, gates_s, p0, p1, pn,
                      new_keys(ks), new_keys(vs), new_keys(kw), new_keys(vw))
    o = o_s.reshape(nb, B_KV_HEADS, GB, nt, HEAD_DIM).transpose(0, 3, 1, 2, 4).reshape(n, WIDTH)
    y = _out_proj(xf, o, z, w_o, n)
    shp = (nb, nt, B_KV_HEADS, 2, HEAD_DIM)
    win_out = jnp.concatenate([state_win, kvw.reshape(shp)], axis=1)[:, nt:]
    return y.reshape(nb, nt, D_MODEL), kvc.reshape(shp), kvs.reshape(shp), win_out


def kernel(x_prompt, x_sample, cache_a_kv, cache_a_idx, cache_b_cmp_kv, cache_b_slc_kv, state_b_win_kv,
           page_table, a_norm, a_w_in, a_q_norm, a_k_norm, a_w_o, b_norm, b_w_in, b_q_norm, b_k_norm,
           b_cmp_w, b_cmp_pe, b_w_o):
    yp, a_kv_p, a_idx_p = _dsa_prompt_layer(x_prompt, a_norm, a_w_in, a_q_norm, a_k_norm, a_w_o)
    yp, b_cmp_p, b_slc_p, b_win_p = _nsa_prompt_layer(yp, b_norm, b_w_in, b_q_norm, b_k_norm,
                                                      b_cmp_w, b_cmp_pe, b_w_o)
    ys, a_kv_s, a_idx_s = _dsa_sample_layer(x_sample, cache_a_kv, cache_a_idx, page_table,
                                            a_norm, a_w_in, a_q_norm, a_k_norm, a_w_o)
    ys, b_cmp_s, b_slc_s, b_win_s = _nsa_sample_layer(ys, cache_b_cmp_kv, cache_b_slc_kv, state_b_win_kv,
                                                      page_table, b_norm, b_w_in, b_q_norm, b_k_norm,
                                                      b_cmp_w, b_cmp_pe, b_w_o)
    return (yp, ys, a_kv_p, a_idx_p, a_kv_s, a_idx_s, b_cmp_p, b_slc_p, b_win_p, b_cmp_s, b_slc_s, b_win_s)
```
